```python
import math
import jax, jax.numpy as jnp
from jax import lax
import numpy as np

D_MODEL = 2048
BATCH = 2
SEQ = 4096
DEPTH = 4
DEC_BATCH = 8
DEC_SEQ = 8
PAST_LEN = 16384
PAGE_SIZE = 128

N_MIXERS = 3
PLE_DIM = 256
EPS = 1e-6
SB_HEADS = 16
SB_HEAD_DIM = 128
SB_WIDTH = SB_HEADS * SB_HEAD_DIM
SB_BLOCK = 128
SB_BIAS_MIN = -12.0
SB_BIAS_MAX = -4.0
CV_WIDTH = D_MODEL
CV_KERNEL = 31
SSD_INNER = 2 * D_MODEL
SSD_HEAD_DIM = 64
SSD_HEADS = SSD_INNER // SSD_HEAD_DIM
SSD_GROUPS = 8
SSD_HPG = SSD_HEADS // SSD_GROUPS
SSD_STATE = 128
SSD_CONV = 4
SSD_CONV_DIM = SSD_INNER + 2 * SSD_GROUPS * SSD_STATE
SSD_CHUNK = 128
N_SB = len(range(0, DEPTH, N_MIXERS))
N_CV = len(range(1, DEPTH, N_MIXERS))
N_SSD = len(range(2, DEPTH, N_MIXERS))

kernel_name = "stickbreak_conformer_ssd_hybrid_step"


def rmsnorm(x, g):
    xf = x.astype(jnp.float32)
    y = xf * lax.rsqrt(jnp.mean(xf * xf, axis=-1, keepdims=True) + EPS)
    return (y * g.astype(jnp.float32)).astype(x.dtype)


def layernorm(x, g, b):
    xf = x.astype(jnp.float32)
    mu = jnp.mean(xf, axis=-1, keepdims=True)
    var = jnp.mean(jnp.square(xf - mu), axis=-1, keepdims=True)
    y = (xf - mu) * lax.rsqrt(var + EPS) * g.astype(jnp.float32) + b.astype(jnp.float32)
    return y.astype(x.dtype)


def causal_dwconv(hist, w, b):
    c = hist.shape[-1]
    out = lax.conv_general_dilated(hist, w[:, None, :].astype(hist.dtype), window_strides=(1,), padding="VALID",
                                   dimension_numbers=("NWC", "WIO", "NWC"), feature_group_count=c)
    return out + b.astype(out.dtype)


def stick_breaking(q, k, v, bias):
    b, tq, h, dh = q.shape
    tk = k.shape[1]
    qb = math.gcd(tq, SB_BLOCK)
    nb = tq // qb
    q_blocks = q.reshape(b, nb, qb, h, dh).swapaxes(0, 1)
    q_pos = (tk - tq + jnp.arange(tq, dtype=jnp.int32)).reshape(nb, qb)
    k_pos = jnp.arange(tk, dtype=jnp.int32)
    scale = 1.0 / math.sqrt(dh)
    bias_f = bias.astype(jnp.float32)[None, :, None, None]

    def block(args):
        qblk, pos = args
        z = jnp.einsum('bqhd,bkhd->bhqk', qblk, k, preferred_element_type=jnp.float32) * scale + bias_f
        mask = k_pos[None, :] < pos[:, None]
        log_keep = jnp.where(mask, jax.nn.log_sigmoid(-z), 0.0)
        log_after = lax.cumsum(log_keep, axis=3, reverse=True) - log_keep
        wts = jnp.where(mask, jnp.exp(jax.nn.log_sigmoid(z) + log_after), 0.0)
        return jnp.einsum('bhqk,bkhd->bqhd', wts.astype(v.dtype), v)

    o = lax.map(block, (q_blocks, q_pos))
    return o.swapaxes(0, 1).reshape(b, tq, h, dh)


def sb_mixer(h, k_past, v_past, w_in, w_out, bias):
    b, t, _ = h.shape
    q, k, v, g = jnp.split(h @ w_in, 4, axis=-1)
    q = q.reshape(b, t, SB_HEADS, SB_HEAD_DIM)
    k = k.reshape(b, t, SB_HEADS, SB_HEAD_DIM)
    v = v.reshape(b, t, SB_HEADS, SB_HEAD_DIM)
    k_all = jnp.concatenate([k_past.astype(k.dtype), k], axis=1)
    v_all = jnp.concatenate([v_past.astype(v.dtype), v], axis=1)
    o = stick_breaking(q, k_all, v_all, bias).reshape(b, t, SB_WIDTH)
    return (o * jax.nn.silu(g)) @ w_out, k, v


def conv_mixer(h, conv_prev, w_in, w_dw, b_dw, ln_g, ln_b, w_out):
    val, gate, z = jnp.split(h @ w_in, 3, axis=-1)
    u = val * jax.nn.sigmoid(gate)
    hist = jnp.concatenate([conv_prev.astype(u.dtype), u], axis=1)
    c = jax.nn.silu(layernorm(causal_dwconv(hist, w_dw, b_dw), ln_g, ln_b))
    return (c * jax.nn.silu(z)) @ w_out, hist[:, -(CV_KERNEL - 1):]


def ssd_scan(x, dt, a, bm, cm, state0):
    b, t, g, r, p = x.shape
    n = bm.shape[-1]
    L = math.gcd(t, SSD_CHUNK)
    nc = t // L
    causal = jnp.tril(jnp.ones((L, L), dtype=bool))[None, :, :, None, None]

    def to_chunks(u):
        return u.reshape(b, nc, L, *u.shape[2:]).swapaxes(0, 1)

    def step(state, inp):
        xc, dtc, bc, cc = inp
        acum = jnp.cumsum(dtc * a, axis=1)
        seg = acum[:, :, None] - acum[:, None]
        decay = jnp.exp(jnp.where(causal, seg, -jnp.inf))
        cb = jnp.einsum('btgn,bsgn->btsg', cc, bc, preferred_element_type=jnp.float32)
        scores = cb[..., None] * decay * dtc[:, None]
        y = jnp.einsum('btsgr,bsgrp->btgrp', scores, xc)
        y = y + jnp.einsum('btgn,bgrpn->btgrp', cc, state) * jnp.exp(acum)[..., None]
        last = acum[:, -1]
        wts = dtc * jnp.exp(last[:, None] - acum)
        state = state * jnp.exp(last)[..., None, None] + jnp.einsum('bsgn,bsgr,bsgrp->bgrpn', bc, wts, xc)
        return state, y

    s0 = state0.astype(jnp.float32).reshape(b, g, r, p, n)
    state, ys = lax.scan(step, s0, (to_chunks(x), to_chunks(dt), to_chunks(bm), to_chunks(cm)))
    y = ys.swapaxes(0, 1).reshape(b, t, g, r, p)
    return y.astype(x.dtype), state.reshape(b, g * r, p, n)


def ssd_mixer(h, conv_prev, ssm_prev, w_in, w_conv, b_conv, dt_bias, a_log, d_skip, norm_g, w_out):
    b, t, _ = h.shape
    z, xbc, dt = jnp.split(h @ w_in, [SSD_INNER, SSD_INNER + SSD_CONV_DIM], axis=-1)
    hist = jnp.concatenate([conv_prev.astype(xbc.dtype), xbc], axis=1)
    xbc = jax.nn.silu(causal_dwconv(hist, w_conv, b_conv))
    x, bm, cm = jnp.split(xbc, [SSD_INNER, SSD_INNER + SSD_GROUPS * SSD_STATE], axis=-1)
    x = x.reshape(b, t, SSD_GROUPS, SSD_HPG, SSD_HEAD_DIM)
    bm = bm.reshape(b, t, SSD_GROUPS, SSD_STATE)
    cm = cm.reshape(b, t, SSD_GROUPS, SSD_STATE)
    dt = jax.nn.softplus(dt.astype(jnp.float32) + dt_bias.astype(jnp.float32)).reshape(b, t, SSD_GROUPS, SSD_HPG)
    a = -jnp.exp(a_log.astype(jnp.float32)).reshape(SSD_GROUPS, SSD_HPG)
    y, ssm_new = ssd_scan(x, dt, a, bm, cm, ssm_prev)
    y = y + d_skip.reshape(SSD_GROUPS, SSD_HPG)[..., None].astype(y.dtype) * x
    y = y.reshape(b, t, SSD_INNER)
    out = rmsnorm(y * jax.nn.silu(z), norm_g) @ w_out
    return out, hist[:, -(SSD_CONV - 1):], ssm_new


def per_layer_embed(x, p, w_gate, w_proj):
    return x + jax.nn.sigmoid(x @ w_gate) * (p @ w_proj)


def setup_inputs(seed: int = 0) -> dict:
    key = jax.random.key(seed)
    ks = iter(jax.random.split(key, 48))
    f32 = jnp.float32

    def nrm(shape, scale=1.0):
        return jax.random.normal(next(ks), shape, f32) * scale

    def dense(shape):
        return nrm(shape, shape[-2] ** -0.5)

    def gain(shape):
        return 1.0 + nrm(shape, 0.02)

    n_pages = PAST_LEN // PAGE_SIZE
    n_used = DEC_BATCH * n_pages
    n_pool = (5 * n_used + 3) // 4
    page_table = jax.random.permutation(next(ks), n_pool)[:n_used].reshape(DEC_BATCH, n_pages).astype(jnp.int32)
    dt0 = jnp.exp(jax.random.uniform(next(ks), (N_SSD, SSD_HEADS), f32, math.log(1e-3), math.log(1e-1)))
    dt_bias = dt0 + jnp.log(-jnp.expm1(-dt0))
    a_log = jnp.log(jax.random.uniform(next(ks), (N_SSD, SSD_HEADS), f32, 1.0, 16.0))
    sb_bias = jax.random.uniform(next(ks), (N_SB, SB_HEADS), f32, SB_BIAS_MIN, SB_BIAS_MAX)
    return {
        "x_prompt": nrm((BATCH, SEQ, D_MODEL)),
        "x_sample": nrm((DEC_BATCH, DEC_SEQ, D_MODEL)),
        "cache_k": nrm((N_SB, n_pool, PAGE_SIZE, SB_HEADS, SB_HEAD_DIM)),
        "cache_v": nrm((N_SB, n_pool, PAGE_SIZE, SB_HEADS, SB_HEAD_DIM)),
        "state_conv": nrm((N_CV, DEC_BATCH, CV_KERNEL - 1, CV_WIDTH), 0.5),
        "state_ssm_conv": nrm((N_SSD, DEC_BATCH, SSD_CONV - 1, SSD_CONV_DIM)),
        "state_ssm": nrm((N_SSD, DEC_BATCH, SSD_HEADS, SSD_HEAD_DIM, SSD_STATE), 0.1),
        "page_table": page_table,
        "p_prompt": nrm((DEPTH, BATCH, SEQ, PLE_DIM)),
        "p_sample": nrm((DEPTH, DEC_BATCH, DEC_SEQ, PLE_DIM)),
        "norm_pre": gain((DEPTH, D_MODEL)),
        "norm_post": gain((DEPTH, D_MODEL)),
        "w_ple_gate": dense((DEPTH, D_MODEL, D_MODEL)),
        "w_ple_proj": dense((DEPTH, PLE_DIM, D_MODEL)),
        "w_sb_in": dense((N_SB, D_MODEL, 4 * SB_WIDTH)),
        "w_sb_out": dense((N_SB, SB_WIDTH, D_MODEL)),
        "sb_bias": sb_bias,
        "w_cv_in": dense((N_CV, D_MODEL, 3 * CV_WIDTH)),
        "w_cv_dw": dense((N_CV, CV_KERNEL, CV_WIDTH)),
        "b_cv_dw": nrm((N_CV, CV_WIDTH), 0.02),
        "ln_cv_g": gain((N_CV, CV_WIDTH)),
        "ln_cv_b": nrm((N_CV, CV_WIDTH), 0.02),
        "w_cv_out": dense((N_CV, CV_WIDTH, D_MODEL)),
        "w_ssd_in": dense((N_SSD, D_MODEL, SSD_INNER + SSD_CONV_DIM + SSD_HEADS)),
        "w_ssd_conv": dense((N_SSD, SSD_CONV, SSD_CONV_DIM)),
        "b_ssd_conv": nrm((N_SSD, SSD_CONV_DIM), 0.02),
        "dt_bias": dt_bias,
        "a_log": a_log,
        "d_skip": 1.0 + nrm((N_SSD, SSD_HEADS), 0.1),
        "norm_ssd": gain((N_SSD, SSD_INNER)),
        "w_ssd_out": dense((N_SSD, SSD_INNER, D_MODEL)),
    }


def reference(x_prompt, x_sample, cache_k, cache_v, state_conv, state_ssm_conv, state_ssm, page_table,
              p_prompt, p_sample, norm_pre, norm_post, w_ple_gate, w_ple_proj, w_sb_in, w_sb_out, sb_bias,
              w_cv_in, w_cv_dw, b_cv_dw, ln_cv_g, ln_cv_b, w_cv_out, w_ssd_in, w_ssd_conv, b_ssd_conv,
              dt_bias, a_log, d_skip, norm_ssd, w_ssd_out):
    xp, xs = x_prompt, x_sample
    bp, bs = xp.shape[0], xs.shape[0]
    k_pr, v_pr, k_sa, v_sa = [], [], [], []
    cv_pr, cv_sa = [], []
    sc_pr, sc_sa, ss_pr, ss_sa = [], [], [], []
    for i in range(DEPTH):
        kind, j = i % N_MIXERS, i // N_MIXERS
        hp = rmsnorm(xp, norm_pre[i])
        hs = rmsnorm(xs, norm_pre[i])
        if kind == 0:
            empty = jnp.zeros((bp, 0, SB_HEADS, SB_HEAD_DIM), xp.dtype)
            mp, kn, vn = sb_mixer(hp, empty, empty, w_sb_in[j], w_sb_out[j], sb_bias[j])
            k_past = cache_k[j][page_table].reshape(bs, -1, SB_HEADS, SB_HEAD_DIM)
            v_past = cache_v[j][page_table].reshape(bs, -1, SB_HEADS, SB_HEAD_DIM)
            ms, kns, vns = sb_mixer(hs, k_past, v_past, w_sb_in[j], w_sb_out[j], sb_bias[j])
            k_pr.append(kn); v_pr.append(vn); k_sa.append(kns); v_sa.append(vns)
        elif kind == 1:
            zero_cv = jnp.zeros((bp, CV_KERNEL - 1, CV_WIDTH), xp.dtype)
            mp, cp = conv_mixer(hp, zero_cv, w_cv_in[j], w_cv_dw[j], b_cv_dw[j], ln_cv_g[j], ln_cv_b[j], w_cv_out[j])
            ms, cs = conv_mixer(hs, state_conv[j], w_cv_in[j], w_cv_dw[j], b_cv_dw[j], ln_cv_g[j], ln_cv_b[j], w_cv_out[j])
            cv_pr.append(cp); cv_sa.append(cs)
        else:
            zero_sc = jnp.zeros((bp, SSD_CONV - 1, SSD_CONV_DIM), xp.dtype)
            zero_ss = jnp.zeros((bp, SSD_HEADS, SSD_HEAD_DIM, SSD_STATE), jnp.float32)
            mp, scp, ssp = ssd_mixer(hp, zero_sc, zero_ss, w_ssd_in[j], w_ssd_conv[j], b_ssd_conv[j],
                                     dt_bias[j], a_log[j], d_skip[j], norm_ssd[j], w_ssd_out[j])
            ms, scs, sss = ssd_mixer(hs, state_ssm_conv[j], state_ssm[j], w_ssd_in[j], w_ssd_conv[j], b_ssd_conv[j],
                                     dt_bias[j], a_log[j], d_skip[j], norm_ssd[j], w_ssd_out[j])
            sc_pr.append(scp); sc_sa.append(scs); ss_pr.append(ssp); ss_sa.append(sss)
        xp = xp + rmsnorm(mp, norm_post[i])
        xs = xs + rmsnorm(ms, norm_post[i])
        xp = per_layer_embed(xp, p_prompt[i], w_ple_gate[i], w_ple_proj[i])
        xs = per_layer_embed(xs, p_sample[i], w_ple_gate[i], w_ple_proj[i])
    return (xp, xs, jnp.stack(k_pr), jnp.stack(v_pr), jnp.stack(k_sa), jnp.stack(v_sa),
            jnp.stack(cv_pr), jnp.stack(cv_sa), jnp.stack(sc_pr), jnp.stack(sc_sa),
            jnp.stack(ss_pr), jnp.stack(ss_sa))
```

```python
import functools
import math

import jax
import jax.numpy as jnp
from jax import lax
from jax.experimental import pallas as pl
from jax.experimental.pallas import tpu as pltpu

F32 = jnp.float32
BF16 = jnp.bfloat16
EPS = 1e-6
N_MIXERS = 3
SSD_STATE = 128
SSD_CHUNK = 128
LANES = 128
SUBLANES = 8
VMEM_LIMIT = 56 * 1024 * 1024


def _params(*sem, vmem=None):
    return pltpu.CompilerParams(dimension_semantics=sem, vmem_limit_bytes=vmem)


def _softplus(z):
    return jnp.maximum(z, 0.0) + jnp.log(1.0 + jnp.exp(-jnp.abs(z)))


def _sigmoid(z):
    return 1.0 / (1.0 + jnp.exp(-z))


def _silu(z):
    return z * _sigmoid(z)


def _dot(a, b):
    return jnp.dot(a, b, preferred_element_type=F32)


def _dot_nt(a, b):
    return lax.dot_general(a, b, (((1,), (1,)), ((), ())), preferred_element_type=F32)


def _split_bf16(x, terms):
    parts = []
    r = x
    for t in range(terms):
        p = r.astype(BF16)
        parts.append(p)
        if t + 1 < terms:
            r = r - p.astype(F32)
    return parts


def _const_spec(shape, single_buffer=False):
    zeros = (0,) * len(shape)
    if single_buffer:
        return pl.BlockSpec(shape, lambda *_: zeros, pipeline_mode=pl.Buffered(1))
    return pl.BlockSpec(shape, lambda *_: zeros)


def _rmsnorm_kernel(x_ref, g_ref, o_ref):
    x = x_ref[...]
    ms = jnp.mean(x * x, axis=-1, keepdims=True)
    o_ref[...] = (x * lax.rsqrt(ms + EPS) * g_ref[...]).astype(o_ref.dtype)


def rmsnorm_bf16(x, g, tm):
    m, d = x.shape
    return pl.pallas_call(
        _rmsnorm_kernel,
        grid=(m // tm,),
        in_specs=[pl.BlockSpec((tm, d), lambda i: (i, 0)), _const_spec((1, d))],
        out_specs=pl.BlockSpec((tm, d), lambda i: (i, 0)),
        out_shape=jax.ShapeDtypeStruct((m, d), BF16),
        compiler_params=_params("parallel"),
        name="rmsnorm",
    )(x, g.reshape(1, d))


def _matmul_kernel(a_ref, w_ref, o_ref):
    o_ref[...] = _dot(a_ref[...], w_ref[...])


def matmul(a, w, tm, tn):
    m, k = a.shape
    n = w.shape[1]
    return pl.pallas_call(
        _matmul_kernel,
        grid=(n // tn, m // tm),
        in_specs=[pl.BlockSpec((tm, k), lambda j, i: (i, 0)), pl.BlockSpec((k, tn), lambda j, i: (0, j))],
        out_specs=pl.BlockSpec((tm, tn), lambda j, i: (i, j)),
        out_shape=jax.ShapeDtypeStruct((m, n), F32),
        compiler_params=_params("parallel", "parallel", vmem=VMEM_LIMIT),
        name="matmul",
    )(a, w)


def _proj_kernel(a_ref, wo_ref, x_ref, g_ref, wg_ref, p_ref, wp_ref, o_ref):
    m = _dot(a_ref[...].astype(BF16), wo_ref[...])
    ms = jnp.mean(m * m, axis=-1, keepdims=True)
    xn = x_ref[...] + m * lax.rsqrt(ms + EPS) * g_ref[...]
    gate = _sigmoid(_dot(xn.astype(BF16), wg_ref[...]))
    pe = _dot(p_ref[...].astype(BF16), wp_ref[...])
    o_ref[...] = xn + gate * pe


def proj_residual_ple(a, w_out, x, g_post, w_gate, p, w_proj, tm):
    m, k = a.shape
    d = x.shape[1]
    e = p.shape[1]
    row = lambda i: (i, 0)
    return pl.pallas_call(
        _proj_kernel,
        grid=(m // tm,),
        in_specs=[
            pl.BlockSpec((tm, k), row),
            _const_spec((k, d), single_buffer=True),
            pl.BlockSpec((tm, d), row),
            _const_spec((1, d)),
            _const_spec((d, d), single_buffer=True),
            pl.BlockSpec((tm, e), row),
            _const_spec((e, d), single_buffer=True),
        ],
        out_specs=pl.BlockSpec((tm, d), row),
        out_shape=jax.ShapeDtypeStruct((m, d), F32),
        compiler_params=_params("parallel", vmem=VMEM_LIMIT),
        name="proj_residual_ple",
    )(a, w_out, x, g_post.reshape(1, d), w_gate, p, w_proj)


def _sb_tile(q, k, v, bias, c, acc, trib, mask):
    tk = k.shape[0]
    z = _dot_nt(q, k) + bias
    lk_raw = -_softplus(z)
    lk = lk_raw if mask is None else jnp.where(mask, lk_raw, 0.0)
    hi, lo = _split_bf16(lk, 2)
    cs = _dot(hi, trib) + _dot(lo, trib)
    w = jnp.exp(z + lk_raw + cs[:, :tk] + c)
    if mask is not None:
        w = jnp.where(mask, w, 0.0)
    acc = acc + _dot(w.astype(BF16), v)
    return c + cs[:, tk:], acc


def _tri_ones(t):
    r = lax.broadcasted_iota(jnp.int32, (t, 2 * t), 0)
    c = lax.broadcasted_iota(jnp.int32, (t, 2 * t), 1)
    return jnp.where((r > c) | (c >= t), 1.0, 0.0).astype(BF16)


def _sb_prompt_kernel(bias_ref, q_ref, k_ref, v_ref, g_ref, o_ref, *, scale, tq):
    h = pl.program_id(1)
    qi = pl.program_id(2)
    bias = bias_ref[h]
    q = (q_ref[...] * scale).astype(BF16)
    dh = q.shape[1]
    row = lax.broadcasted_iota(jnp.int32, (tq, tq), 0)
    col = lax.broadcasted_iota(jnp.int32, (tq, tq), 1)
    trib = _tri_ones(tq)

    def kv(kb):
        start = pl.multiple_of(kb * tq, tq)
        return k_ref[pl.ds(start, tq), :].astype(BF16), v_ref[pl.ds(start, tq), :].astype(BF16)

    k, v = kv(qi)
    c, acc = _sb_tile(q, k, v, bias, jnp.zeros((tq, tq), F32), jnp.zeros((tq, dh), F32), trib, col < row)

    def body(i, carry):
        k, v = kv(qi - 1 - i)
        return _sb_tile(q, k, v, bias, carry[0], carry[1], trib, None)

    c, acc = lax.fori_loop(0, qi, body, (c, acc))
    o_ref[...] = (acc * _silu(g_ref[...])).astype(o_ref.dtype)


def sb_attention_prompt(q, k, v, g, bias, batch, heads, tq=128):
    n, width = q.shape
    t = n // batch
    dh = width // heads
    nq = t // tq
    qspec = pl.BlockSpec((tq, dh), lambda b, h, i: (b * nq + i, h))
    kvspec = pl.BlockSpec((t, dh), lambda b, h, i: (b, h))
    return pl.pallas_call(
        functools.partial(_sb_prompt_kernel, scale=1.0 / math.sqrt(dh), tq=tq),
        grid=(batch, heads, nq),
        in_specs=[pl.BlockSpec(memory_space=pltpu.SMEM), qspec, kvspec, kvspec, qspec],
        out_specs=qspec,
        out_shape=jax.ShapeDtypeStruct((n, width), BF16),
        compiler_params=_params("parallel", "parallel", "parallel"),
        name="sb_attention_prompt",
    )(bias, q, k, v, g)


def _sb_sample_kernel(pt_ref, qbd_ref, bias_ref, kn_ref, vn_ref, kc_ref, vc_ref, g_ref, o_ref, c_ref, acc_ref,
                      *, heads, tq):
    s = pl.program_id(1)
    page = kc_ref.shape[0]
    hq = heads * tq
    r2 = lax.broadcasted_iota(jnp.int32, (2 * page, page), 0)
    c2 = lax.broadcasted_iota(jnp.int32, (2 * page, page), 1)
    trib = jnp.where((c2 > r2) | (r2 >= page), 1.0, 0.0).astype(BF16)

    def tile(k_ref, v_ref, mask):
        z = _dot(k_ref[...].astype(BF16), qbd_ref[...]) + bias_ref[...]
        lk_raw = -_softplus(z)
        lk = lk_raw if mask is None else jnp.where(mask, lk_raw, 0.0)
        hi, lo = _split_bf16(lk, 2)
        cs = _dot(trib, hi) + _dot(trib, lo)
        c = c_ref[...]
        w = jnp.exp(z + lk_raw + cs[:page] + c[0:1, :])
        if mask is not None:
            w = jnp.where(mask, w, 0.0)
        acc_ref[...] += _dot(w.T.astype(BF16), v_ref[...].astype(BF16))
        c_ref[...] = c + cs[page:page + SUBLANES]

    @pl.when(s == 0)
    def _():
        c_ref[...] = jnp.zeros_like(c_ref)
        acc_ref[...] = jnp.zeros_like(acc_ref)
        key = lax.broadcasted_iota(jnp.int32, (page, hq), 0)
        qry = lax.broadcasted_iota(jnp.int32, (page, hq), 1) % tq
        tile(kn_ref, vn_ref, key < qry)

    @pl.when(s > 0)
    def _():
        tile(kc_ref, vc_ref, None)

    @pl.when(s == pl.num_programs(1) - 1)
    def _():
        dh = acc_ref.shape[1] // heads
        for h in range(heads):
            cols = slice(h * dh, (h + 1) * dh)
            o_ref[:, cols] = acc_ref[h * tq:(h + 1) * tq, cols] * _silu(g_ref[:, cols])


def sb_attention_sample(q, k, v, g, bias, cache_k, cache_v, layer, page_table, heads):
    seqs, tq, width = q.shape
    dh = width // heads
    hq = heads * tq
    page = cache_k.shape[2]
    n_pages = page_table.shape[1]
    assert hq == LANES and tq == SUBLANES and tq <= page
    scale = 1.0 / math.sqrt(dh)
    q4 = (q * scale).reshape(seqs, tq, heads, dh)
    eye = jnp.eye(heads, dtype=F32)
    qbd = jnp.einsum("bqhd,hg->bhdgq", q4, eye).reshape(seqs, width, hq).astype(BF16)
    bias_row = jnp.repeat(bias, tq).reshape(1, hq)
    pad = ((0, 0), (0, page - tq), (0, 0))
    kn = jnp.pad(k, pad)
    vn = jnp.pad(v, pad)
    seq_spec = lambda shape: pl.BlockSpec((None,) + shape, lambda b, s, pt: (b, 0, 0))
    cache_spec = pl.BlockSpec((None, None, page, width),
                              lambda b, s, pt: (layer, pt[b, n_pages - jnp.maximum(s, 1)], 0, 0))
    grid_spec = pltpu.PrefetchScalarGridSpec(
        num_scalar_prefetch=1,
        grid=(seqs, n_pages + 1),
        in_specs=[seq_spec((width, hq)), pl.BlockSpec((1, hq), lambda b, s, pt: (0, 0)),
                  seq_spec((page, width)), seq_spec((page, width)), cache_spec, cache_spec, seq_spec((tq, width))],
        out_specs=seq_spec((tq, width)),
        scratch_shapes=[pltpu.VMEM((SUBLANES, hq), F32), pltpu.VMEM((hq, width), F32)],
    )
    return pl.pallas_call(
        functools.partial(_sb_sample_kernel, heads=heads, tq=tq),
        grid_spec=grid_spec,
        out_shape=jax.ShapeDtypeStruct((seqs, tq, width), F32),
        compiler_params=_params("parallel", "arbitrary"),
        name="sb_attention_sample",
    )(page_table, qbd, bias_row, kn, vn, cache_k, cache_v, g)


CV_HIST = 32


def _conv_kernel(val_ref, gate_ref, z_ref, st_ref, w_ref, b_ref, lg_ref, lb_ref, a_ref, ns_ref, e_ref, *, taps, tt):
    ti = pl.program_id(1)
    off = CV_HIST - (taps - 1)

    @pl.when(ti == 0)
    def _():
        e_ref[0:CV_HIST, :] = st_ref[...]

    e_ref[CV_HIST:CV_HIST + tt, :] = val_ref[...] * _sigmoid(gate_ref[...])
    acc = jnp.broadcast_to(b_ref[...], (tt, b_ref.shape[1]))
    for w in range(taps):
        acc = acc + e_ref[off + w:off + w + tt, :] * w_ref[w:w + 1, :]
    mu = jnp.mean(acc, axis=-1, keepdims=True)
    cen = acc - mu
    var = jnp.mean(cen * cen, axis=-1, keepdims=True)
    c = _silu(cen * lax.rsqrt(var + EPS) * lg_ref[...] + lb_ref[...])
    a_ref[...] = (c * _silu(z_ref[...])).astype(a_ref.dtype)

    @pl.when(ti == pl.num_programs(1) - 1)
    def _():
        ns_ref[...] = e_ref[tt + off:tt + CV_HIST, :]

    e_ref[0:CV_HIST, :] = e_ref[tt:tt + CV_HIST, :]


def conv_module(val, gate, z, state, w_dw, b_dw, ln_g, ln_b, batch, tt, out_dtype):
    n, c = val.shape
    taps = w_dw.shape[0]
    nt = n // batch // tt
    st = jnp.pad(state, ((0, 0), (CV_HIST - (taps - 1), 0), (0, 0)))
    row = pl.BlockSpec((tt, c), lambda b, i: (b * nt + i, 0))
    vec = _const_spec((1, c))
    return pl.pallas_call(
        functools.partial(_conv_kernel, taps=taps, tt=tt),
        grid=(batch, nt),
        in_specs=[row, row, row, pl.BlockSpec((None, CV_HIST, c), lambda b, i: (b, 0, 0)),
                  _const_spec((taps, c)), vec, vec, vec],
        out_specs=[row, pl.BlockSpec((None, taps - 1, c), lambda b, i: (b, 0, 0))],
        out_shape=[jax.ShapeDtypeStruct((n, c), out_dtype), jax.ShapeDtypeStruct((batch, taps - 1, c), F32)],
        scratch_shapes=[pltpu.VMEM((CV_HIST + tt, c), F32)],
        compiler_params=_params("parallel", "arbitrary", vmem=VMEM_LIMIT),
        name="conv_module",
    )(val, gate, z, st, w_dw, b_dw.reshape(1, c), ln_g.reshape(1, c), ln_b.reshape(1, c))


SSD_HIST = 8


def _ssd_kernel(z_ref, xbc_ref, dt_ref, cst_ref, sst_ref, wc_ref, bc_ref, dtb_ref, alog_ref, dsk_ref, ng_ref,
                a_ref, ncs_ref, nss_ref, e_ref, act_ref, y_ref, s_ref, acum_t_ref, dt_t_ref,
                *, taps, inner, groups, hpg, hd, t_valid):
    ci = pl.program_id(1)
    L = z_ref.shape[0]
    n = SSD_STATE
    gw = hpg * hd
    off = SSD_HIST - (taps - 1)

    @pl.when(ci == 0)
    def _():
        e_ref[0:SSD_HIST, :] = cst_ref[...]
        s_ref[...] = sst_ref[...]

    e_ref[SSD_HIST:SSD_HIST + L, :] = xbc_ref[...]
    conv = jnp.broadcast_to(bc_ref[...], (L, bc_ref.shape[1]))
    for w in range(taps):
        conv = conv + e_ref[off + w:off + w + L, :] * wc_ref[w:w + 1, :]
    act_ref[...] = _silu(conv)

    rows = lax.broadcasted_iota(jnp.int32, (L, LANES), 0) + ci * L
    dt = jnp.where(rows < t_valid, _softplus(dt_ref[...] + dtb_ref[...]), 0.0)
    da = dt * (-jnp.exp(alog_ref[...]))
    r = lax.broadcasted_iota(jnp.int32, (L, L), 0)
    c = lax.broadcasted_iota(jnp.int32, (L, L), 1)
    causal = r >= c
    tril = jnp.where(causal, 1.0, 0.0).astype(BF16)
    acum = sum(_dot(tril, part) for part in _split_bf16(da, 3))
    acum_t_ref[...] = acum.T
    dt_t_ref[...] = dt.T
    lane = lax.broadcasted_iota(jnp.int32, (L, LANES), 1)
    sub = lax.broadcasted_iota(jnp.int32, (LANES, n), 0)
    first = lane < hd
    first_rows = sub < hd

    def group(g, _):
        b_g = act_ref[:, pl.ds(pl.multiple_of(inner + g * n, n), n)].astype(BF16)
        c_g = act_ref[:, pl.ds(pl.multiple_of(inner + groups * n + g * n, n), n)].astype(BF16)
        cb = _dot_nt(c_g, b_g)
        for i in range(hpg // 2):
            h0 = g * hpg + 2 * i
            xcols = pl.ds(pl.multiple_of(g * gw + i * LANES, LANES), LANES)
            xp = act_ref[:, xcols]
            y = dsk_ref[:, xcols] * xp
            cols, rws, lasts = [], [], []
            for j in range(2):
                hh = h0 + j
                a_col = jnp.sum(jnp.where(lane == hh, acum, 0.0), axis=1, keepdims=True)
                a_row = acum_t_ref[pl.ds(hh, 1), :]
                dt_row = dt_t_ref[pl.ds(hh, 1), :]
                dec = jnp.exp(jnp.where(causal, a_col - a_row, -jnp.inf))
                sc = (cb * dec * dt_row).astype(BF16)
                xm = jnp.where(first if j == 0 else ~first, xp, 0.0).astype(BF16)
                y = y + _dot(sc, xm)
                last = a_row[:, L - 1:L]
                cols.append(a_col)
                rws.append(dt_row * jnp.exp(last - a_row))
                lasts.append(jnp.exp(last))
            srows = pl.ds(pl.multiple_of(h0 * hd, LANES), LANES)
            st = s_ref[srows, :]
            y = y + _dot_nt(c_g, st.astype(BF16)) * jnp.exp(jnp.where(first, cols[0], cols[1]))
            y_ref[:, xcols] = y
            xw_t = xp.T * jnp.where(first_rows, rws[0], rws[1])
            s_ref[srows, :] = st * jnp.where(first_rows, lasts[0], lasts[1]) + _dot(xw_t.astype(BF16), b_g)
        return 0

    lax.fori_loop(0, groups, group, 0)

    yz = y_ref[...] * _silu(z_ref[...])
    ms = jnp.mean(yz * yz, axis=-1, keepdims=True)
    a_ref[...] = (yz * lax.rsqrt(ms + EPS) * ng_ref[...]).astype(a_ref.dtype)

    @pl.when(ci == pl.num_programs(1) - 1)
    def _():
        tv = t_valid - (-(-t_valid // L) - 1) * L
        ncs_ref[...] = e_ref[SSD_HIST + tv - (taps - 1):SSD_HIST + tv, :]
        nss_ref[...] = s_ref[...]

    e_ref[0:SSD_HIST, :] = e_ref[L:L + SSD_HIST, :]


def ssd_module(z, xbc, dt, conv_state, ssm_state, w_conv, b_conv, dt_bias, a_log, d_skip, norm_g, batch, t_valid):
    nrow, inner = z.shape
    conv_dim = xbc.shape[1]
    taps = w_conv.shape[0]
    heads = dt_bias.shape[0]
    hd = inner // heads
    groups = (conv_dim - inner) // (2 * SSD_STATE)
    hpg = heads // groups
    L = SSD_CHUNK
    nc = nrow // batch // L
    assert 2 * hd == LANES and hpg % 2 == 0 and heads <= LANES and L == LANES
    padh = lambda v: jnp.pad(v, (0, LANES - heads)).reshape(1, LANES)
    cst = jnp.pad(conv_state, ((0, 0), (SSD_HIST - (taps - 1), 0), (0, 0)))
    sst = ssm_state.reshape(batch, heads * hd, SSD_STATE)
    row = lambda w: pl.BlockSpec((L, w), lambda b, i: (b * nc + i, 0))
    per_b = lambda shape: pl.BlockSpec((None,) + shape, lambda b, i: (b, 0, 0))
    return pl.pallas_call(
        functools.partial(_ssd_kernel, taps=taps, inner=inner, groups=groups, hpg=hpg, hd=hd, t_valid=t_valid),
        grid=(batch, nc),
        in_specs=[row(inner), row(conv_dim), row(LANES), per_b((SSD_HIST, conv_dim)), per_b((heads * hd, SSD_STATE)),
                  _const_spec((taps, conv_dim)), _const_spec((1, conv_dim)), _const_spec((1, LANES)),
                  _const_spec((1, LANES)), _const_spec((1, inner)), _const_spec((1, inner))],
        out_specs=[row(inner), per_b((taps - 1, conv_dim)), per_b((heads * hd, SSD_STATE))],
        out_shape=[jax.ShapeDtypeStruct((nrow, inner), BF16),
                   jax.ShapeDtypeStruct((batch, taps - 1, conv_dim), F32),
                   jax.ShapeDtypeStruct((batch, heads * hd, SSD_STATE), F32)],
        scratch_shapes=[pltpu.VMEM((SSD_HIST + L, conv_dim), F32), pltpu.VMEM((L, conv_dim), F32),
                        pltpu.VMEM((L, inner), F32), pltpu.VMEM((heads * hd, SSD_STATE), F32),
                        pltpu.VMEM((LANES, L), F32), pltpu.VMEM((LANES, L), F32)],
        compiler_params=_params("parallel", "arbitrary", vmem=VMEM_LIMIT),
        name="ssd_module",
    )(z, xbc, dt, cst, sst, w_conv, b_conv.reshape(1, conv_dim), padh(dt_bias), padh(a_log),
      jnp.repeat(d_skip, hd).reshape(1, inner), norm_g.reshape(1, inner))


def _tile(m, pref):
    return pref if m % pref == 0 else m


def kernel(x_prompt, x_sample, cache_k, cache_v, state_conv, state_ssm_conv, state_ssm, page_table, p_prompt, p_sample, norm_pre, norm_post, w_ple_gate, w_ple_proj, w_sb_in, w_sb_out, sb_bias, w_cv_in, w_cv_dw, b_cv_dw, ln_cv_g, ln_cv_b, w_cv_out, w_ssd_in, w_ssd_conv, b_ssd_conv, dt_bias, a_log, d_skip, norm_ssd, w_ssd_out):
    bp, tp, d = x_prompt.shape
    bs, ts, _ = x_sample.shape
    depth = norm_pre.shape[0]
    heads = sb_bias.shape[1]
    sb_width = w_sb_out.shape[1]
    dh = sb_width // heads
    page = cache_k.shape[2]
    cv_width = w_cv_out.shape[1]
    inner = norm_ssd.shape[1]
    conv_dim = w_ssd_conv.shape[2]
    ssd_heads = dt_bias.shape[1]
    L = SSD_CHUNK

    np_, ns_ = bp * tp, bs * ts
    xp = x_prompt.reshape(np_, d)
    xs = x_sample.reshape(ns_, d)
    cache_k2 = cache_k.reshape(cache_k.shape[0], cache_k.shape[1], page, sb_width)
    cache_v2 = cache_v.reshape(cache_v.shape[0], cache_v.shape[1], page, sb_width)
    tmp = _tile(np_, 512)
    tpp = _tile(np_, 256)
    bf = lambda w: w.astype(BF16)

    def project(h, w, tm):
        return matmul(h, w, tm, _tile(w.shape[1], 2048))

    k_pr, v_pr, k_sa, v_sa, cv_pr, cv_sa, sc_pr, sc_sa, ss_pr, ss_sa = ([] for _ in range(10))
    for i in range(depth):
        kind, j = i % N_MIXERS, i // N_MIXERS
        hp = rmsnorm_bf16(xp, norm_pre[i], tmp)
        hs = rmsnorm_bf16(xs, norm_pre[i], ns_)
        if kind == 0:
            w_in = [bf(w_sb_in[j][:, n * sb_width:(n + 1) * sb_width]) for n in range(4)]
            qp, kp, vp, gp = (project(hp, w, tmp) for w in w_in)
            ap = sb_attention_prompt(qp, kp, vp, gp, sb_bias[j], bp, heads)
            qs, ks, vs, gs = (project(hs, w, ns_).reshape(bs, ts, sb_width) for w in w_in)
            as_ = sb_attention_sample(qs, ks, vs, gs, sb_bias[j], cache_k2, cache_v2, j, page_table, heads)
            as_ = as_.reshape(ns_, sb_width)
            w_out = bf(w_sb_out[j])
            k_pr.append(kp.reshape(bp, tp, heads, dh)); v_pr.append(vp.reshape(bp, tp, heads, dh))
            k_sa.append(ks.reshape(bs, ts, heads, dh)); v_sa.append(vs.reshape(bs, ts, heads, dh))
        elif kind == 1:
            w_in = [bf(w_cv_in[j][:, n * cv_width:(n + 1) * cv_width]) for n in range(3)]
            taps = w_cv_dw.shape[1]
            valp, gatep, zp = (project(hp, w, tmp) for w in w_in)
            zero_cv = jnp.zeros((bp, taps - 1, cv_width), F32)
            ap, cp = conv_module(valp, gatep, zp, zero_cv, w_cv_dw[j], b_cv_dw[j], ln_cv_g[j], ln_cv_b[j],
                                 bp, _tile(tp, 256), BF16)
            vals, gates, zs = (project(hs, w, ns_) for w in w_in)
            as_, cs = conv_module(vals, gates, zs, state_conv[j], w_cv_dw[j], b_cv_dw[j], ln_cv_g[j], ln_cv_b[j],
                                  bs, ts, F32)
            w_out = bf(w_cv_out[j])
            cv_pr.append(cp); cv_sa.append(cs)
        else:
            w = w_ssd_in[j]
            w_z = bf(w[:, :inner])
            w_xbc = bf(w[:, inner:inner + conv_dim])
            w_dt = bf(jnp.pad(w[:, inner + conv_dim:], ((0, 0), (0, LANES - ssd_heads))))
            taps = w_ssd_conv.shape[1]
            args = (w_ssd_conv[j], b_ssd_conv[j], dt_bias[j], a_log[j], d_skip[j], norm_ssd[j])
            zp, xbcp, dtp = (project(hp, ww, tmp) for ww in (w_z, w_xbc, w_dt))
            zero_sc = jnp.zeros((bp, taps - 1, conv_dim), F32)
            zero_ss = jnp.zeros((bp, ssd_heads, inner // ssd_heads, SSD_STATE), F32)
            ap, scp, ssp = ssd_module(zp, xbcp, dtp, zero_sc, zero_ss, *args, bp, tp)
            padrows = lambda u: jnp.pad(u.reshape(bs, ts, -1), ((0, 0), (0, L - ts), (0, 0))).reshape(bs * L, -1)
            zs, xbcs, dts = (padrows(project(hs, ww, ns_)) for ww in (w_z, w_xbc, w_dt))
            as_, scs, sss = ssd_module(zs, xbcs, dts, state_ssm_conv[j], state_ssm[j], *args, bs, ts)
            as_ = as_.reshape(bs, L, inner)[:, :ts].reshape(ns_, inner)
            w_out = bf(w_ssd_out[j])
            shape5 = (-1, ssd_heads, inner // ssd_heads, SSD_STATE)
            sc_pr.append(scp); sc_sa.append(scs); ss_pr.append(ssp.reshape(shape5)); ss_sa.append(sss.reshape(shape5))
        w_gate, w_proj = bf(w_ple_gate[i]), bf(w_ple_proj[i])
        xp = proj_residual_ple(ap, w_out, xp, norm_post[i], w_gate, p_prompt[i].reshape(np_, -1), w_proj, tpp)
        xs = proj_residual_ple(as_, w_out, xs, norm_post[i], w_gate, p_sample[i].reshape(ns_, -1), w_proj, ns_)
    return (xp.reshape(bp, tp, d), xs.reshape(bs, ts, d), jnp.stack(k_pr), jnp.stack(v_pr), jnp.stack(k_sa),
            jnp.stack(v_sa), jnp.stack(cv_pr), jnp.stack(cv_sa), jnp.stack(sc_pr), jnp.stack(sc_sa),
            jnp.stack(ss_pr), jnp.stack(ss_sa))
```

```python
import functools
import math

import jax
import jax.numpy as jnp
from jax import lax
from jax.experimental import pallas as pl
from jax.experimental.pallas import tpu as pltpu

F32 = jnp.float32
BF16 = jnp.bfloat16
EPS = 1e-6
N_MIXERS = 3
SSD_STATE = 128
SSD_CHUNK = 128
LANES = 128
SUBLANES = 8
VMEM_LIMIT = 56 * 1024 * 1024


def _params(*sem, vmem=None):
    return pltpu.CompilerParams(dimension_semantics=sem, vmem_limit_bytes=vmem)


def _softplus(z):
    return jnp.maximum(z, 0.0) + jnp.log(1.0 + jnp.exp(-jnp.abs(z)))


def _sigmoid(z):
    return 1.0 / (1.0 + jnp.exp(-z))


def _silu(z):
    return z * _sigmoid(z)


def _dot(a, b):
    return jnp.dot(a, b, preferred_element_type=F32)


def _dot_nt(a, b):
    return lax.dot_general(a, b, (((1,), (1,)), ((), ())), preferred_element_type=F32)


def _split_bf16(x, terms):
    parts = []
    r = x
    for t in range(terms):
        p = r.astype(BF16)
        parts.append(p)
        if t + 1 < terms:
            r = r - p.astype(F32)
    return parts


def _const_spec(shape, single_buffer=False):
    zeros = (0,) * len(shape)
    if single_buffer:
        return pl.BlockSpec(shape, lambda *_: zeros, pipeline_mode=pl.Buffered(1))
    return pl.BlockSpec(shape, lambda *_: zeros)


def _rmsnorm_kernel(x_ref, g_ref, o_ref):
    x = x_ref[...]
    ms = jnp.mean(x * x, axis=-1, keepdims=True)
    o_ref[...] = (x * lax.rsqrt(ms + EPS) * g_ref[...]).astype(o_ref.dtype)


def rmsnorm_bf16(x, g, tm):
    m, d = x.shape
    return pl.pallas_call(
        _rmsnorm_kernel,
        grid=(m // tm,),
        in_specs=[pl.BlockSpec((tm, d), lambda i: (i, 0)), _const_spec((1, d))],
        out_specs=pl.BlockSpec((tm, d), lambda i: (i, 0)),
        out_shape=jax.ShapeDtypeStruct((m, d), BF16),
        compiler_params=_params("parallel"),
        name="rmsnorm",
    )(x, g.reshape(1, d))


def _matmul_kernel(a_ref, w_ref, *o_refs):
    r = _dot(a_ref[...], w_ref[...])
    for o_ref in o_refs:
        o_ref[...] = r.astype(o_ref.dtype)


def matmul(a, w, tm, tn, dtypes=(F32,)):
    m, k = a.shape
    n = w.shape[1]
    out = pl.pallas_call(
        _matmul_kernel,
        grid=(n // tn, m // tm),
        in_specs=[pl.BlockSpec((tm, k), lambda j, i: (i, 0)), pl.BlockSpec((k, tn), lambda j, i: (0, j))],
        out_specs=[pl.BlockSpec((tm, tn), lambda j, i: (i, j)) for _ in dtypes],
        out_shape=[jax.ShapeDtypeStruct((m, n), dt) for dt in dtypes],
        compiler_params=_params("parallel", "parallel", vmem=VMEM_LIMIT),
        name="matmul",
    )(a, w)
    return out[0] if len(dtypes) == 1 else out


def _proj_kernel(a_ref, wo_ref, x_ref, g_ref, wg_ref, p_ref, wp_ref, o_ref):
    m = _dot(a_ref[...].astype(BF16), wo_ref[...])
    ms = jnp.mean(m * m, axis=-1, keepdims=True)
    xn = x_ref[...] + m * lax.rsqrt(ms + EPS) * g_ref[...]
    gate = _sigmoid(_dot(xn.astype(BF16), wg_ref[...]))
    pe = _dot(p_ref[...].astype(BF16), wp_ref[...])
    o_ref[...] = xn + gate * pe


def proj_residual_ple(a, w_out, x, g_post, w_gate, p, w_proj, tm):
    m, k = a.shape
    d = x.shape[1]
    e = p.shape[1]
    row = lambda i: (i, 0)
    return pl.pallas_call(
        _proj_kernel,
        grid=(m // tm,),
        in_specs=[
            pl.BlockSpec((tm, k), row),
            _const_spec((k, d), single_buffer=True),
            pl.BlockSpec((tm, d), row),
            _const_spec((1, d)),
            _const_spec((d, d), single_buffer=True),
            pl.BlockSpec((tm, e), row),
            _const_spec((e, d), single_buffer=True),
        ],
        out_specs=pl.BlockSpec((tm, d), row),
        out_shape=jax.ShapeDtypeStruct((m, d), F32),
        compiler_params=_params("parallel", vmem=VMEM_LIMIT),
        name="proj_residual_ple",
    )(a, w_out, x, g_post.reshape(1, d), w_gate, p, w_proj)


def _sb_scores(q, k, bias, trib, mask):
    tk = k.shape[0]
    z = _dot_nt(q, k) + bias
    lk_raw = -_softplus(z)
    lk = lk_raw if mask is None else jnp.where(mask, lk_raw, 0.0)
    cs = _dot(jnp.concatenate(_split_bf16(lk, 2), axis=1), trib)
    return z + lk_raw + cs[:, :tk], cs[:, tk:]


def _sb_accumulate(logw, total, v, c, acc, mask):
    w = jnp.exp(logw + c)
    if mask is not None:
        w = jnp.where(mask, w, 0.0)
    return c + total, acc + _dot(w.astype(BF16), v)


def _tri_ones(t):
    r = lax.broadcasted_iota(jnp.int32, (2 * t, 2 * t), 0) % t
    c = lax.broadcasted_iota(jnp.int32, (2 * t, 2 * t), 1)
    return jnp.where((r > c) | (c >= t), 1.0, 0.0).astype(BF16)


def _sb_prompt_kernel(bias_ref, q_ref, k_ref, v_ref, g_ref, o_ref, z_ref, w_ref, c_ref, acc_ref, *, scale, tq, tk):
    h = pl.program_id(1)
    qi = pl.program_id(2)
    bias = bias_ref[h]
    nd = tq // tk
    trib = _tri_ones(tk)
    q = (q_ref[...] * scale).astype(BF16)

    def kv(ref, kb):
        return ref[pl.ds(pl.multiple_of(kb * tk, tk), tk), :]

    c = jnp.zeros((tq, tk), F32)
    acc = jnp.zeros((tq, q.shape[1]), F32)
    for d in reversed(range(nd)):
        r0 = d * tk
        row = lax.broadcasted_iota(jnp.int32, (tq - r0, tk), 0)
        col = lax.broadcasted_iota(jnp.int32, (tq - r0, tk), 1)
        mask = col < row
        logw, total = _sb_scores(q[r0:], kv(k_ref, qi * nd + d), bias, trib, mask)
        c_new, acc_new = _sb_accumulate(logw, total, kv(v_ref, qi * nd + d), c[r0:], acc[r0:], mask)
        c = jnp.concatenate([c[:r0], c_new], axis=0) if r0 else c_new
        acc = jnp.concatenate([acc[:r0], acc_new], axis=0) if r0 else acc_new
    c_ref[...] = c
    acc_ref[...] = acc

    n = qi * nd
    tile = lambda i: jnp.maximum(n - 1 - i, 0)

    def logits(i):
        z_ref[...] = _dot_nt(q, kv(k_ref, tile(i))) + bias

    def weights():
        z = z_ref[...]
        lk = -_softplus(z)
        cs = _dot(jnp.concatenate(_split_bf16(lk, 2), axis=1), trib)
        c = c_ref[...]
        w_ref[...] = jnp.exp(z + lk + cs[:, :tk] + c).astype(BF16)
        c_ref[...] = c + cs[:, tk:]

    def accumulate(i):
        acc_ref[...] += _dot(w_ref[...], kv(v_ref, tile(i)))

    @pl.when(n > 0)
    def _():
        logits(0)
        weights()
        logits(1)

    def body(i, _):
        accumulate(i)
        weights()
        logits(i + 2)
        return 0

    lax.fori_loop(0, n, body, 0)
    o_ref[...] = (acc_ref[...] * _silu(g_ref[...])).astype(o_ref.dtype)


def sb_attention_prompt(q, k, v, g, bias, batch, heads, tq, tk=LANES):
    n, width = q.shape
    t = n // batch
    dh = width // heads
    nq = t // tq
    qspec = pl.BlockSpec((tq, dh), lambda b, h, i: (b * nq + i, h))
    kvspec = pl.BlockSpec((t, dh), lambda b, h, i: (b, h))
    return pl.pallas_call(
        functools.partial(_sb_prompt_kernel, scale=1.0 / math.sqrt(dh), tq=tq, tk=tk),
        grid=(batch, heads, nq),
        in_specs=[pl.BlockSpec(memory_space=pltpu.SMEM), qspec, kvspec, kvspec, qspec],
        out_specs=qspec,
        out_shape=jax.ShapeDtypeStruct((n, width), BF16),
        scratch_shapes=[pltpu.VMEM((tq, tk), F32), pltpu.VMEM((tq, tk), BF16), pltpu.VMEM((tq, tk), F32),
                        pltpu.VMEM((tq, dh), F32)],
        compiler_params=_params("parallel", "parallel", "parallel"),
        name="sb_attention_prompt",
    )(bias, q, k, v, g)


def _sb_sample_kernel(pt_ref, qbd_ref, bias_ref, kn_ref, vn_ref, *refs, heads, tq, pages_per_step):
    kc_refs = refs[:pages_per_step]
    vc_refs = refs[pages_per_step:2 * pages_per_step]
    g_ref, o_ref, c_ref, acc_ref = refs[2 * pages_per_step:]
    s = pl.program_id(1)
    page = kn_ref.shape[0] // heads
    dh = kn_ref.shape[1]
    hq = heads * tq
    r2 = lax.broadcasted_iota(jnp.int32, (2 * page, 2 * page), 0)
    c2 = lax.broadcasted_iota(jnp.int32, (2 * page, 2 * page), 1) % page
    trib = jnp.where((c2 > r2) | (r2 >= page), 1.0, 0.0).astype(BF16)

    def head_rows(ref, h):
        return ref[pl.ds(h, page, stride=heads), :].astype(BF16)

    def tile(k_ref, v_ref, mask):
        z = bias_ref[...] + sum(_dot(head_rows(k_ref, h), qbd_ref[h * dh:(h + 1) * dh, :]) for h in range(heads))
        lk_raw = -_softplus(z)
        lk = lk_raw if mask is None else jnp.where(mask, lk_raw, 0.0)
        cs = _dot(trib, jnp.concatenate(_split_bf16(lk, 2), axis=0))
        c = c_ref[...]
        w = jnp.exp(z + lk_raw + cs[:page] + c[0:1, :])
        if mask is not None:
            w = jnp.where(mask, w, 0.0)
        wt = w.T.astype(BF16)
        for h in range(heads):
            acc_ref[:, h * dh:(h + 1) * dh] += _dot(wt[h * tq:(h + 1) * tq, :], head_rows(v_ref, h))
        c_ref[...] = c + cs[page:page + SUBLANES]

    @pl.when(s == 0)
    def _():
        c_ref[...] = jnp.zeros_like(c_ref)
        acc_ref[...] = jnp.zeros_like(acc_ref)
        key = lax.broadcasted_iota(jnp.int32, (page, hq), 0)
        qry = lax.broadcasted_iota(jnp.int32, (page, hq), 1) % tq
        tile(kn_ref, vn_ref, key < qry)

    @pl.when(s > 0)
    def _():
        for k_ref, v_ref in zip(kc_refs, vc_refs):
            tile(k_ref, v_ref, None)

    @pl.when(s == pl.num_programs(1) - 1)
    def _():
        o_ref[...] = acc_ref[...] * _silu(g_ref[...])


def sb_attention_sample(q, k, v, g, bias, cache_k, cache_v, layer, page_table, pages_per_step):
    seqs, tq, width = q.shape
    _, pool, page, heads, dh = cache_k.shape
    hq = heads * tq
    n_pages = page_table.shape[1]
    assert hq == LANES and tq == SUBLANES and tq <= page and n_pages % pages_per_step == 0
    cache_k = cache_k.reshape(-1, pool, page * heads, dh)
    cache_v = cache_v.reshape(-1, pool, page * heads, dh)
    scale = 1.0 / math.sqrt(dh)
    q4 = (q * scale).reshape(seqs, tq, heads, dh)
    eye = jnp.eye(heads, dtype=F32)
    qbd = jnp.einsum("bqhd,hg->bhdgq", q4, eye).reshape(seqs, width, hq).astype(BF16)
    bias_row = jnp.repeat(bias, tq).reshape(1, hq)
    pad = lambda u: jnp.pad(u, ((0, 0), (0, page - tq), (0, 0))).reshape(seqs, page * heads, dh)
    seq_spec = lambda shape: pl.BlockSpec((None,) + shape, lambda b, s, pt: (b, 0, 0))

    def cache_spec(p):
        def index(b, s, pt):
            return layer, pt[b, n_pages - 1 - ((jnp.maximum(s, 1) - 1) * pages_per_step + p)], 0, 0
        return pl.BlockSpec((None, None, page * heads, dh), index)

    cache_specs = [cache_spec(p) for p in range(pages_per_step)]
    grid_spec = pltpu.PrefetchScalarGridSpec(
        num_scalar_prefetch=1,
        grid=(seqs, n_pages // pages_per_step + 1),
        in_specs=[seq_spec((width, hq)), pl.BlockSpec((1, hq), lambda b, s, pt: (0, 0)),
                  seq_spec((page * heads, dh)), seq_spec((page * heads, dh)), *cache_specs, *cache_specs,
                  seq_spec((tq, width))],
        out_specs=seq_spec((tq, width)),
        scratch_shapes=[pltpu.VMEM((SUBLANES, hq), F32), pltpu.VMEM((tq, width), F32)],
    )
    return pl.pallas_call(
        functools.partial(_sb_sample_kernel, heads=heads, tq=tq, pages_per_step=pages_per_step),
        grid_spec=grid_spec,
        out_shape=jax.ShapeDtypeStruct((seqs, tq, width), F32),
        compiler_params=_params("parallel", "arbitrary", vmem=VMEM_LIMIT),
        name="sb_attention_sample",
    )(page_table, qbd, bias_row, pad(k), pad(v), *([cache_k] * pages_per_step), *([cache_v] * pages_per_step), g)


CV_HIST = 32


def _conv_kernel(val_ref, gate_ref, z_ref, st_ref, w_ref, b_ref, lg_ref, lb_ref, a_ref, ns_ref, e_ref, *, taps, tt):
    ti = pl.program_id(1)
    off = CV_HIST - (taps - 1)

    @pl.when(ti == 0)
    def _():
        e_ref[0:CV_HIST, :] = st_ref[...]

    e_ref[CV_HIST:CV_HIST + tt, :] = val_ref[...] * _sigmoid(gate_ref[...])
    acc = jnp.broadcast_to(b_ref[...], (tt, b_ref.shape[1]))
    for w in range(taps):
        acc = acc + e_ref[off + w:off + w + tt, :] * w_ref[w:w + 1, :]
    mu = jnp.mean(acc, axis=-1, keepdims=True)
    cen = acc - mu
    var = jnp.mean(cen * cen, axis=-1, keepdims=True)
    c = _silu(cen * lax.rsqrt(var + EPS) * lg_ref[...] + lb_ref[...])
    a_ref[...] = (c * _silu(z_ref[...])).astype(a_ref.dtype)

    @pl.when(ti == pl.num_programs(1) - 1)
    def _():
        ns_ref[...] = e_ref[tt + off:tt + CV_HIST, :]

    e_ref[0:CV_HIST, :] = e_ref[tt:tt + CV_HIST, :]


def conv_module(val, gate, z, state, w_dw, b_dw, ln_g, ln_b, batch, tt, out_dtype):
    n, c = val.shape
    taps = w_dw.shape[0]
    nt = n // batch // tt
    st = jnp.pad(state, ((0, 0), (CV_HIST - (taps - 1), 0), (0, 0)))
    row = pl.BlockSpec((tt, c), lambda b, i: (b * nt + i, 0))
    vec = _const_spec((1, c))
    return pl.pallas_call(
        functools.partial(_conv_kernel, taps=taps, tt=tt),
        grid=(batch, nt),
        in_specs=[row, row, row, pl.BlockSpec((None, CV_HIST, c), lambda b, i: (b, 0, 0)),
                  _const_spec((taps, c)), vec, vec, vec],
        out_specs=[row, pl.BlockSpec((None, taps - 1, c), lambda b, i: (b, 0, 0))],
        out_shape=[jax.ShapeDtypeStruct((n, c), out_dtype), jax.ShapeDtypeStruct((batch, taps - 1, c), F32)],
        scratch_shapes=[pltpu.VMEM((CV_HIST + tt, c), F32)],
        compiler_params=_params("parallel", "arbitrary", vmem=VMEM_LIMIT),
        name="conv_module",
    )(val, gate, z, st, w_dw, b_dw.reshape(1, c), ln_g.reshape(1, c), ln_b.reshape(1, c))


SSD_HIST = 8


def _ssd_kernel(z_ref, xbc_ref, dt_ref, cst_ref, sst_ref, wc_ref, bc_ref, dtb_ref, alog_ref, dsk_ref, ng_ref,
                a_ref, ncs_ref, nss_ref, e_ref, act_ref, y_ref, s_ref, acum_t_ref, dt_t_ref,
                *, taps, inner, groups, hpg, hd, t_valid):
    ci = pl.program_id(1)
    L = z_ref.shape[0]
    n = SSD_STATE
    gw = hpg * hd
    off = SSD_HIST - (taps - 1)

    @pl.when(ci == 0)
    def _():
        e_ref[0:SSD_HIST, :] = cst_ref[...]
        s_ref[...] = sst_ref[...]

    e_ref[SSD_HIST:SSD_HIST + L, :] = xbc_ref[...]
    conv = jnp.broadcast_to(bc_ref[...], (L, bc_ref.shape[1]))
    for w in range(taps):
        conv = conv + e_ref[off + w:off + w + L, :] * wc_ref[w:w + 1, :]
    act_ref[...] = _silu(conv)

    rows = lax.broadcasted_iota(jnp.int32, (L, LANES), 0) + ci * L
    dt = jnp.where(rows < t_valid, _softplus(dt_ref[...] + dtb_ref[...]), 0.0)
    da = dt * (-jnp.exp(alog_ref[...]))
    r = lax.broadcasted_iota(jnp.int32, (L, L), 0)
    c = lax.broadcasted_iota(jnp.int32, (L, L), 1)
    causal = r >= c
    tril = jnp.where(causal, 1.0, 0.0).astype(BF16)
    acum = sum(_dot(tril, part) for part in _split_bf16(da, 3))
    acum_t_ref[...] = acum.T
    dt_t_ref[...] = dt.T
    lane = lax.broadcasted_iota(jnp.int32, (L, LANES), 1)
    sub = lax.broadcasted_iota(jnp.int32, (LANES, n), 0)
    first = lane < hd
    first_rows = sub < hd

    def group(g, _):
        b_g = act_ref[:, pl.ds(pl.multiple_of(inner + g * n, n), n)].astype(BF16)
        c_g = act_ref[:, pl.ds(pl.multiple_of(inner + groups * n + g * n, n), n)].astype(BF16)
        cb = _dot_nt(c_g, b_g)
        for i in range(hpg // 2):
            h0 = g * hpg + 2 * i
            xcols = pl.ds(pl.multiple_of(g * gw + i * LANES, LANES), LANES)
            xp = act_ref[:, xcols]
            y = dsk_ref[:, xcols] * xp
            cols, rws, lasts = [], [], []
            for j in range(2):
                hh = h0 + j
                a_col = jnp.sum(jnp.where(lane == hh, acum, 0.0), axis=1, keepdims=True)
                a_row = acum_t_ref[pl.ds(hh, 1), :]
                dt_row = dt_t_ref[pl.ds(hh, 1), :]
                dec = jnp.exp(jnp.where(causal, a_col - a_row, -jnp.inf))
                sc = (cb * dec * dt_row).astype(BF16)
                xm = jnp.where(first if j == 0 else ~first, xp, 0.0).astype(BF16)
                y = y + _dot(sc, xm)
                last = a_row[:, L - 1:L]
                cols.append(a_col)
                rws.append(dt_row * jnp.exp(last - a_row))
                lasts.append(jnp.exp(last))
            srows = pl.ds(pl.multiple_of(h0 * hd, LANES), LANES)
            st = s_ref[srows, :]
            y = y + _dot_nt(c_g, st.astype(BF16)) * jnp.exp(jnp.where(first, cols[0], cols[1]))
            y_ref[:, xcols] = y
            xw_t = xp.T * jnp.where(first_rows, rws[0], rws[1])
            s_ref[srows, :] = st * jnp.where(first_rows, lasts[0], lasts[1]) + _dot(xw_t.astype(BF16), b_g)
        return 0

    lax.fori_loop(0, groups, group, 0)

    yz = y_ref[...] * _silu(z_ref[...])
    ms = jnp.mean(yz * yz, axis=-1, keepdims=True)
    a_ref[...] = (yz * lax.rsqrt(ms + EPS) * ng_ref[...]).astype(a_ref.dtype)

    @pl.when(ci == pl.num_programs(1) - 1)
    def _():
        tv = t_valid - (-(-t_valid // L) - 1) * L
        ncs_ref[...] = e_ref[SSD_HIST + tv - (taps - 1):SSD_HIST + tv, :]
        nss_ref[...] = s_ref[...]

    e_ref[0:SSD_HIST, :] = e_ref[L:L + SSD_HIST, :]


def ssd_module(z, xbc, dt, conv_state, ssm_state, w_conv, b_conv, dt_bias, a_log, d_skip, norm_g, batch, t_valid):
    nrow, inner = z.shape
    conv_dim = xbc.shape[1]
    taps = w_conv.shape[0]
    heads = dt_bias.shape[0]
    hd = inner // heads
    groups = (conv_dim - inner) // (2 * SSD_STATE)
    hpg = heads // groups
    L = SSD_CHUNK
    nc = nrow // batch // L
    assert 2 * hd == LANES and hpg % 2 == 0 and heads <= LANES and L == LANES
    padh = lambda v: jnp.pad(v, (0, LANES - heads)).reshape(1, LANES)
    cst = jnp.pad(conv_state, ((0, 0), (SSD_HIST - (taps - 1), 0), (0, 0)))
    sst = ssm_state.reshape(batch, heads * hd, SSD_STATE)
    row = lambda w: pl.BlockSpec((L, w), lambda b, i: (b * nc + i, 0))
    per_b = lambda shape: pl.BlockSpec((None,) + shape, lambda b, i: (b, 0, 0))
    return pl.pallas_call(
        functools.partial(_ssd_kernel, taps=taps, inner=inner, groups=groups, hpg=hpg, hd=hd, t_valid=t_valid),
        grid=(batch, nc),
        in_specs=[row(inner), row(conv_dim), row(LANES), per_b((SSD_HIST, conv_dim)), per_b((heads * hd, SSD_STATE)),
                  _const_spec((taps, conv_dim)), _const_spec((1, conv_dim)), _const_spec((1, LANES)),
                  _const_spec((1, LANES)), _const_spec((1, inner)), _const_spec((1, inner))],
        out_specs=[row(inner), per_b((taps - 1, conv_dim)), per_b((heads * hd, SSD_STATE))],
        out_shape=[jax.ShapeDtypeStruct((nrow, inner), BF16),
                   jax.ShapeDtypeStruct((batch, taps - 1, conv_dim), F32),
                   jax.ShapeDtypeStruct((batch, heads * hd, SSD_STATE), F32)],
        scratch_shapes=[pltpu.VMEM((SSD_HIST + L, conv_dim), F32), pltpu.VMEM((L, conv_dim), F32),
                        pltpu.VMEM((L, inner), F32), pltpu.VMEM((heads * hd, SSD_STATE), F32),
                        pltpu.VMEM((LANES, L), F32), pltpu.VMEM((LANES, L), F32)],
        compiler_params=_params("parallel", "arbitrary", vmem=VMEM_LIMIT),
        name="ssd_module",
    )(z, xbc, dt, cst, sst, w_conv, b_conv.reshape(1, conv_dim), padh(dt_bias), padh(a_log),
      jnp.repeat(d_skip, hd).reshape(1, inner), norm_g.reshape(1, inner))


def _tile(m, pref):
    return pref if m % pref == 0 else m


def kernel(x_prompt, x_sample, cache_k, cache_v, state_conv, state_ssm_conv, state_ssm, page_table, p_prompt, p_sample, norm_pre, norm_post, w_ple_gate, w_ple_proj, w_sb_in, w_sb_out, sb_bias, w_cv_in, w_cv_dw, b_cv_dw, ln_cv_g, ln_cv_b, w_cv_out, w_ssd_in, w_ssd_conv, b_ssd_conv, dt_bias, a_log, d_skip, norm_ssd, w_ssd_out):
    bp, tp, d = x_prompt.shape
    bs, ts, _ = x_sample.shape
    depth = norm_pre.shape[0]
    heads = sb_bias.shape[1]
    sb_width = w_sb_out.shape[1]
    dh = sb_width // heads
    page = cache_k.shape[2]
    cv_width = w_cv_out.shape[1]
    inner = norm_ssd.shape[1]
    conv_dim = w_ssd_conv.shape[2]
    ssd_heads = dt_bias.shape[1]
    L = SSD_CHUNK

    np_, ns_ = bp * tp, bs * ts
    xp = x_prompt.reshape(np_, d)
    xs = x_sample.reshape(ns_, d)
    tmp = _tile(np_, 512)
    tpp = _tile(np_, 256)
    bf = lambda w: w.astype(BF16)

    def project(h, w, tm, dtypes=(F32,)):
        return matmul(h, w, tm, _tile(w.shape[1], 2048), dtypes)

    k_pr, v_pr, k_sa, v_sa, cv_pr, cv_sa, sc_pr, sc_sa, ss_pr, ss_sa = ([] for _ in range(10))
    for i in range(depth):
        kind, j = i % N_MIXERS, i // N_MIXERS
        hp = rmsnorm_bf16(xp, norm_pre[i], tmp)
        hs = rmsnorm_bf16(xs, norm_pre[i], ns_)
        if kind == 0:
            w_in = [bf(w_sb_in[j][:, n * sb_width:(n + 1) * sb_width]) for n in range(4)]
            qp, gp = project(hp, w_in[0], tmp), project(hp, w_in[3], tmp)
            (kp, kp16), (vp, vp16) = (project(hp, w, tmp, (F32, BF16)) for w in w_in[1:3])
            ap = sb_attention_prompt(qp, kp16, vp16, gp, sb_bias[j], bp, heads, _tile(tp, 512))
            qs, ks, vs, gs = (project(hs, w, ns_).reshape(bs, ts, sb_width) for w in w_in)
            as_ = sb_attention_sample(qs, ks, vs, gs, sb_bias[j], cache_k, cache_v, j, page_table,
                                      math.gcd(page_table.shape[1], 4))
            as_ = as_.reshape(ns_, sb_width)
            w_out = bf(w_sb_out[j])
            k_pr.append(kp.reshape(bp, tp, heads, dh)); v_pr.append(vp.reshape(bp, tp, heads, dh))
            k_sa.append(ks.reshape(bs, ts, heads, dh)); v_sa.append(vs.reshape(bs, ts, heads, dh))
        elif kind == 1:
            w_in = [bf(w_cv_in[j][:, n * cv_width:(n + 1) * cv_width]) for n in range(3)]
            taps = w_cv_dw.shape[1]
            valp, gatep, zp = (project(hp, w, tmp) for w in w_in)
            zero_cv = jnp.zeros((bp, taps - 1, cv_width), F32)
            ap, cp = conv_module(valp, gatep, zp, zero_cv, w_cv_dw[j], b_cv_dw[j], ln_cv_g[j], ln_cv_b[j],
                                 bp, _tile(tp, 256), BF16)
            vals, gates, zs = (project(hs, w, ns_) for w in w_in)
            as_, cs = conv_module(vals, gates, zs, state_conv[j], w_cv_dw[j], b_cv_dw[j], ln_cv_g[j], ln_cv_b[j],
                                  bs, ts, F32)
            w_out = bf(w_cv_out[j])
            cv_pr.append(cp); cv_sa.append(cs)
        else:
            w = w_ssd_in[j]
            w_z = bf(w[:, :inner])
            w_xbc = bf(w[:, inner:inner + conv_dim])
            w_dt = bf(jnp.pad(w[:, inner + conv_dim:], ((0, 0), (0, LANES - ssd_heads))))
            taps = w_ssd_conv.shape[1]
            args = (w_ssd_conv[j], b_ssd_conv[j], dt_bias[j], a_log[j], d_skip[j], norm_ssd[j])
            zp, xbcp, dtp = (project(hp, ww, tmp) for ww in (w_z, w_xbc, w_dt))
            zero_sc = jnp.zeros((bp, taps - 1, conv_dim), F32)
            zero_ss = jnp.zeros((bp, ssd_heads, inner // ssd_heads, SSD_STATE), F32)
            ap, scp, ssp = ssd_module(zp, xbcp, dtp, zero_sc, zero_ss, *args, bp, tp)
            padrows = lambda u: jnp.pad(u.reshape(bs, ts, -1), ((0, 0), (0, L - ts), (0, 0))).reshape(bs * L, -1)
            zs, xbcs, dts = (padrows(project(hs, ww, ns_)) for ww in (w_z, w_xbc, w_dt))
            as_, scs, sss = ssd_module(zs, xbcs, dts, state_ssm_conv[j], state_ssm[j], *args, bs, ts)
            as_ = as_.reshape(bs, L, inner)[:, :ts].reshape(ns_, inner)
            w_out = bf(w_ssd_out[j])
            shape5 = (-1, ssd_heads, inner // ssd_heads, SSD_STATE)
            sc_pr.append(scp); sc_sa.append(scs); ss_pr.append(ssp.reshape(shape5)); ss_sa.append(sss.reshape(shape5))
        w_gate, w_proj = bf(w_ple_gate[i]), bf(w_ple_proj[i])
        xp = proj_residual_ple(ap, w_out, xp, norm_post[i], w_gate, p_prompt[i].reshape(np_, -1), w_proj, tpp)
        xs = proj_residual_ple(as_, w_out, xs, norm_post[i], w_gate, p_sample[i].reshape(ns_, -1), w_proj, ns_)
    return (xp.reshape(bp, tp, d), xs.reshape(bs, ts, d), jnp.stack(k_pr), jnp.stack(v_pr), jnp.stack(k_sa),
            jnp.stack(v_sa), jnp.stack(cv_pr), jnp.stack(cv_sa), jnp.stack(sc_pr), jnp.stack(sc_sa),
            jnp.stack(ss_pr), jnp.stack(ss_sa))
```

```python
import functools
import math

import jax
import jax.numpy as jnp
from jax import lax
from jax.experimental import pallas as pl
from jax.experimental.pallas import tpu as pltpu

F32 = jnp.float32
BF16 = jnp.bfloat16
EPS = 1e-6
LOG2E = math.log2(math.e)
N_MIXERS = 3
SSD_STATE = 128
SSD_CHUNK = 128
LANES = 128
SUBLANES = 8
VMEM_LIMIT = 56 * 1024 * 1024


def _params(*sem, vmem=None):
    return pltpu.CompilerParams(dimension_semantics=sem, vmem_limit_bytes=vmem)


def _softplus(z):
    return jnp.maximum(z, 0.0) + jnp.log(1.0 + jnp.exp(-jnp.abs(z)))


def _neg_abs(z):
    sign = jnp.uint32(0x80000000)
    return lax.bitcast_convert_type(lax.bitcast_convert_type(z, jnp.uint32) | sign, F32)


def _softplus2(z2):
    return jnp.maximum(z2, 0.0) + jnp.log(1.0 + jnp.exp2(_neg_abs(z2))) * LOG2E


def _sigmoid(z):
    return 1.0 / (1.0 + jnp.exp(-z))


def _silu(z):
    return z * _sigmoid(z)


def _dot(a, b):
    return jnp.dot(a, b, preferred_element_type=F32)


def _dot_nt(a, b):
    return lax.dot_general(a, b, (((1,), (1,)), ((), ())), preferred_element_type=F32)


def _split_bf16(x, terms):
    parts = []
    r = x
    for t in range(terms):
        p = r.astype(BF16)
        parts.append(p)
        if t + 1 < terms:
            r = r - p.astype(F32)
    return parts


def _const_spec(shape, single_buffer=False):
    zeros = (0,) * len(shape)
    if single_buffer:
        return pl.BlockSpec(shape, lambda *_: zeros, pipeline_mode=pl.Buffered(1))
    return pl.BlockSpec(shape, lambda *_: zeros)


def _rmsnorm_kernel(x_ref, g_ref, o_ref):
    x = x_ref[...]
    ms = jnp.mean(x * x, axis=-1, keepdims=True)
    o_ref[...] = (x * lax.rsqrt(ms + EPS) * g_ref[...]).astype(o_ref.dtype)


def rmsnorm_bf16(x, g, tm):
    m, d = x.shape
    return pl.pallas_call(
        _rmsnorm_kernel,
        grid=(m // tm,),
        in_specs=[pl.BlockSpec((tm, d), lambda i: (i, 0)), _const_spec((1, d))],
        out_specs=pl.BlockSpec((tm, d), lambda i: (i, 0)),
        out_shape=jax.ShapeDtypeStruct((m, d), BF16),
        compiler_params=_params("parallel"),
        name="rmsnorm",
    )(x, g.reshape(1, d))


def _matmul_kernel(a_ref, w_ref, *o_refs):
    r = _dot(a_ref[...], w_ref[...])
    for o_ref in o_refs:
        o_ref[...] = r.astype(o_ref.dtype)


def matmul(a, w, tm, tn, dtypes=(F32,)):
    m, k = a.shape
    n = w.shape[1]
    out = pl.pallas_call(
        _matmul_kernel,
        grid=(n // tn, m // tm),
        in_specs=[pl.BlockSpec((tm, k), lambda j, i: (i, 0)), pl.BlockSpec((k, tn), lambda j, i: (0, j))],
        out_specs=[pl.BlockSpec((tm, tn), lambda j, i: (i, j)) for _ in dtypes],
        out_shape=[jax.ShapeDtypeStruct((m, n), dt) for dt in dtypes],
        compiler_params=_params("parallel", "parallel", vmem=VMEM_LIMIT),
        name="matmul",
    )(a, w)
    return out[0] if len(dtypes) == 1 else out


def _proj_kernel(a_ref, wo_ref, x_ref, g_ref, wg_ref, p_ref, wp_ref, o_ref):
    m = _dot(a_ref[...].astype(BF16), wo_ref[...])
    ms = jnp.mean(m * m, axis=-1, keepdims=True)
    xn = x_ref[...] + m * lax.rsqrt(ms + EPS) * g_ref[...]
    gate = _sigmoid(_dot(xn.astype(BF16), wg_ref[...]))
    pe = _dot(p_ref[...].astype(BF16), wp_ref[...])
    o_ref[...] = xn + gate * pe


def proj_residual_ple(a, w_out, x, g_post, w_gate, p, w_proj, tm):
    m, k = a.shape
    d = x.shape[1]
    e = p.shape[1]
    row = lambda i: (i, 0)
    return pl.pallas_call(
        _proj_kernel,
        grid=(m // tm,),
        in_specs=[
            pl.BlockSpec((tm, k), row),
            _const_spec((k, d), single_buffer=True),
            pl.BlockSpec((tm, d), row),
            _const_spec((1, d)),
            _const_spec((d, d), single_buffer=True),
            pl.BlockSpec((tm, e), row),
            _const_spec((e, d), single_buffer=True),
        ],
        out_specs=pl.BlockSpec((tm, d), row),
        out_shape=jax.ShapeDtypeStruct((m, d), F32),
        compiler_params=_params("parallel", vmem=VMEM_LIMIT),
        name="proj_residual_ple",
    )(a, w_out, x, g_post.reshape(1, d), w_gate, p, w_proj)


def _sb_scores(z2, trib, mask):
    tk = z2.shape[1]
    sp = _softplus2(z2)
    drop = sp if mask is None else jnp.where(mask, sp, 0.0)
    cs = _dot(drop.astype(BF16), trib)
    if trib.shape[1] == tk:
        total = jnp.broadcast_to(-jnp.sum(drop, axis=1, keepdims=True), (z2.shape[0], LANES))
        return z2 - sp + cs, total
    return z2 - sp + cs[:, :tk], cs[:, tk:]


def _tri_ones(t, total_width):
    r = lax.broadcasted_iota(jnp.int32, (t, t + total_width), 0)
    c = lax.broadcasted_iota(jnp.int32, (t, t + total_width), 1)
    return jnp.where((r > c) | (c >= t), -1.0, 0.0).astype(BF16)


def _sb_prompt_kernel(bias_ref, q_ref, k_ref, v_ref, g_ref, o_ref, z_ref, w_ref, c_ref, acc_ref, *, scale, tq, tk):
    h = pl.program_id(1)
    qi = pl.program_id(2)
    bias = bias_ref[h] * LOG2E
    nd = tq // tk
    trib = _tri_ones(tk, 0)
    q = (q_ref[...] * (scale * LOG2E)).astype(BF16)
    c_ref[...] = jnp.zeros_like(c_ref)
    acc_ref[...] = jnp.zeros_like(acc_ref)

    n = (qi + 1) * nd

    def kv(ref, i):
        kb = jnp.maximum(n - 1 - i, 0)
        return ref[pl.ds(pl.multiple_of(kb * tk, tk), tk), :]

    def logits(i):
        z_ref[...] = _dot_nt(q, kv(k_ref, i)) + bias

    def weights(i, masked):
        mask = None
        if masked:
            row = lax.broadcasted_iota(jnp.int32, (tq, tk), 0)
            col = lax.broadcasted_iota(jnp.int32, (tq, tk), 1)
            mask = col + (nd - 1 - i) * tk < row
        logw, total = _sb_scores(z_ref[...], trib, mask)
        c = c_ref[...]
        w = jnp.exp2(logw + jnp.concatenate([c] * (tk // c.shape[1]), axis=1))
        w_ref[...] = (w if mask is None else jnp.where(mask, w, 0.0)).astype(BF16)
        c_ref[...] = c + total

    def accumulate(i):
        acc_ref[...] += _dot(w_ref[...], kv(v_ref, i))

    logits(0)
    weights(0, True)
    logits(1)
    for i in range(nd - 1):
        accumulate(i)
        weights(i + 1, True)
        logits(i + 2)

    def body(i, _):
        accumulate(i)
        weights(i + 1, False)
        logits(i + 2)
        return 0

    lax.fori_loop(nd - 1, n, body, 0)
    o_ref[...] = (acc_ref[...] * _silu(g_ref[...])).astype(o_ref.dtype)


def sb_attention_prompt(q, k, v, g, bias, batch, heads, tq, tk):
    n, width = q.shape
    t = n // batch
    dh = width // heads
    nq = t // tq
    assert tq % tk == 0 and tk % LANES == 0
    qspec = pl.BlockSpec((tq, dh), lambda b, h, i: (b * nq + i, h))
    kvspec = pl.BlockSpec((t, dh), lambda b, h, i: (b, h))
    return pl.pallas_call(
        functools.partial(_sb_prompt_kernel, scale=1.0 / math.sqrt(dh), tq=tq, tk=tk),
        grid=(batch, heads, nq),
        in_specs=[pl.BlockSpec(memory_space=pltpu.SMEM), qspec, kvspec, kvspec, qspec],
        out_specs=qspec,
        out_shape=jax.ShapeDtypeStruct((n, width), BF16),
        scratch_shapes=[pltpu.VMEM((tq, tk), F32), pltpu.VMEM((tq, tk), BF16), pltpu.VMEM((tq, LANES), F32),
                        pltpu.VMEM((tq, dh), F32)],
        compiler_params=_params("parallel", "parallel", "parallel"),
        name="sb_attention_prompt",
    )(bias, q, k, v, g)


def _sb_sample_kernel(pt_ref, qbd_ref, bias_ref, kn_ref, vn_ref, *refs, heads, tq, pages_per_step):
    kc_refs = refs[:pages_per_step]
    vc_refs = refs[pages_per_step:2 * pages_per_step]
    g_ref, o_ref, c_ref, acc_ref = refs[2 * pages_per_step:]
    s = pl.program_id(1)
    page = kn_ref.shape[0] // heads
    dh = kn_ref.shape[1]
    hq = heads * tq
    r2 = lax.broadcasted_iota(jnp.int32, (2 * page, page), 0)
    c2 = lax.broadcasted_iota(jnp.int32, (2 * page, page), 1)
    trib = jnp.where((c2 > r2) | (r2 >= page), -1.0, 0.0).astype(BF16)

    def head_rows(ref, h):
        return ref[pl.ds(h, page, stride=heads), :].astype(BF16)

    def tile(k_ref, v_ref, mask):
        z2 = bias_ref[...] + sum(_dot(head_rows(k_ref, h), qbd_ref[h * dh:(h + 1) * dh, :]) for h in range(heads))
        sp = _softplus2(z2)
        drop = sp if mask is None else jnp.where(mask, sp, 0.0)
        cs = _dot(trib, drop.astype(BF16))
        c = c_ref[...]
        w = jnp.exp2(z2 - sp + cs[:page] + c[0:1, :])
        if mask is not None:
            w = jnp.where(mask, w, 0.0)
        wt = w.T.astype(BF16)
        for h in range(heads):
            acc_ref[:, h * dh:(h + 1) * dh] += _dot(wt[h * tq:(h + 1) * tq, :], head_rows(v_ref, h))
        c_ref[...] = c + cs[page:page + SUBLANES]

    @pl.when(s == 0)
    def _():
        c_ref[...] = jnp.zeros_like(c_ref)
        acc_ref[...] = jnp.zeros_like(acc_ref)
        key = lax.broadcasted_iota(jnp.int32, (page, hq), 0)
        qry = lax.broadcasted_iota(jnp.int32, (page, hq), 1) % tq
        tile(kn_ref, vn_ref, key < qry)

    @pl.when(s > 0)
    def _():
        for k_ref, v_ref in zip(kc_refs, vc_refs):
            tile(k_ref, v_ref, None)

    @pl.when(s == pl.num_programs(1) - 1)
    def _():
        o_ref[...] = acc_ref[...] * _silu(g_ref[...])


def sb_attention_sample(q, k, v, g, bias, cache_k, cache_v, layer, page_table, pages_per_step):
    seqs, tq, width = q.shape
    _, pool, page, heads, dh = cache_k.shape
    hq = heads * tq
    n_pages = page_table.shape[1]
    assert hq == LANES and tq == SUBLANES and tq <= page and n_pages % pages_per_step == 0
    cache_k = cache_k.reshape(-1, pool, page * heads, dh)
    cache_v = cache_v.reshape(-1, pool, page * heads, dh)
    scale = LOG2E / math.sqrt(dh)
    q4 = (q * scale).reshape(seqs, tq, heads, dh)
    eye = jnp.eye(heads, dtype=F32)
    qbd = jnp.einsum("bqhd,hg->bhdgq", q4, eye).reshape(seqs, width, hq).astype(BF16)
    bias_row = jnp.repeat(bias * LOG2E, tq).reshape(1, hq)
    pad = lambda u: jnp.pad(u, ((0, 0), (0, page - tq), (0, 0))).reshape(seqs, page * heads, dh)
    seq_spec = lambda shape: pl.BlockSpec((None,) + shape, lambda b, s, pt: (b, 0, 0))

    def cache_spec(p):
        def index(b, s, pt):
            return layer, pt[b, n_pages - 1 - ((jnp.maximum(s, 1) - 1) * pages_per_step + p)], 0, 0
        return pl.BlockSpec((None, None, page * heads, dh), index)

    cache_specs = [cache_spec(p) for p in range(pages_per_step)]
    grid_spec = pltpu.PrefetchScalarGridSpec(
        num_scalar_prefetch=1,
        grid=(seqs, n_pages // pages_per_step + 1),
        in_specs=[seq_spec((width, hq)), pl.BlockSpec((1, hq), lambda b, s, pt: (0, 0)),
                  seq_spec((page * heads, dh)), seq_spec((page * heads, dh)), *cache_specs, *cache_specs,
                  seq_spec((tq, width))],
        out_specs=seq_spec((tq, width)),
        scratch_shapes=[pltpu.VMEM((SUBLANES, hq), F32), pltpu.VMEM((tq, width), F32)],
    )
    return pl.pallas_call(
        functools.partial(_sb_sample_kernel, heads=heads, tq=tq, pages_per_step=pages_per_step),
        grid_spec=grid_spec,
        out_shape=jax.ShapeDtypeStruct((seqs, tq, width), F32),
        compiler_params=_params("parallel", "arbitrary", vmem=VMEM_LIMIT),
        name="sb_attention_sample",
    )(page_table, qbd, bias_row, pad(k), pad(v), *([cache_k] * pages_per_step), *([cache_v] * pages_per_step), g)


CV_HIST = 32


def _conv_kernel(val_ref, gate_ref, z_ref, st_ref, w_ref, b_ref, lg_ref, lb_ref, a_ref, ns_ref, e_ref, co_ref,
                 win_ref, *, taps, tt):
    ti = pl.program_id(1)
    off = CV_HIST - (taps - 1)

    @pl.when(ti == 0)
    def _():
        e_ref[0:CV_HIST, :] = st_ref[...]

    e_ref[CV_HIST:CV_HIST + tt, :] = val_ref[...] * _sigmoid(gate_ref[...])
    rb = min(tt, LANES)

    def column_block(ci, _):
        cols = pl.ds(pl.multiple_of(ci * LANES, LANES), LANES)
        for r0 in range(0, tt, rb):
            acc = jnp.broadcast_to(b_ref[:, cols], (rb, LANES))
            for s in range(SUBLANES):
                ws = [w for w in range(taps) if (off + w) % SUBLANES == s]
                if not ws:
                    continue
                span = rb + ws[-1] - ws[0]
                win_ref[0:span, :] = e_ref[pl.ds(off + ws[0] + r0, span), cols]
                for w in ws:
                    acc = acc + win_ref[w - ws[0]:w - ws[0] + rb, :] * w_ref[pl.ds(w, 1), cols]
            co_ref[pl.ds(r0, rb), cols] = acc
        return 0

    lax.fori_loop(0, e_ref.shape[1] // LANES, column_block, 0)
    acc = co_ref[...]
    mu = jnp.mean(acc, axis=-1, keepdims=True)
    cen = acc - mu
    var = jnp.mean(cen * cen, axis=-1, keepdims=True)
    c = _silu(cen * lax.rsqrt(var + EPS) * lg_ref[...] + lb_ref[...])
    a_ref[...] = (c * _silu(z_ref[...])).astype(a_ref.dtype)

    @pl.when(ti == pl.num_programs(1) - 1)
    def _():
        ns_ref[...] = e_ref[tt + off:tt + CV_HIST, :]

    e_ref[0:CV_HIST, :] = e_ref[tt:tt + CV_HIST, :]


def conv_module(val, gate, z, state, w_dw, b_dw, ln_g, ln_b, batch, tt, out_dtype):
    n, c = val.shape
    taps = w_dw.shape[0]
    nt = n // batch // tt
    st = jnp.pad(state, ((0, 0), (CV_HIST - (taps - 1), 0), (0, 0)))
    row = pl.BlockSpec((tt, c), lambda b, i: (b * nt + i, 0))
    vec = _const_spec((1, c))
    return pl.pallas_call(
        functools.partial(_conv_kernel, taps=taps, tt=tt),
        grid=(batch, nt),
        in_specs=[row, row, row, pl.BlockSpec((None, CV_HIST, c), lambda b, i: (b, 0, 0)),
                  _const_spec((taps, c)), vec, vec, vec],
        out_specs=[row, pl.BlockSpec((None, taps - 1, c), lambda b, i: (b, 0, 0))],
        out_shape=[jax.ShapeDtypeStruct((n, c), out_dtype), jax.ShapeDtypeStruct((batch, taps - 1, c), F32)],
        scratch_shapes=[pltpu.VMEM((CV_HIST + tt, c), F32), pltpu.VMEM((tt, c), F32),
                        pltpu.VMEM((min(tt, LANES) + CV_HIST, LANES), F32)],
        compiler_params=_params("parallel", "arbitrary", vmem=VMEM_LIMIT),
        name="conv_module",
    )(val, gate, z, st, w_dw, b_dw.reshape(1, c), ln_g.reshape(1, c), ln_b.reshape(1, c))


SSD_HIST = 8


def _ssd_kernel(z_ref, xbc_ref, dt_ref, cst_ref, sst_ref, wc_ref, bc_ref, dtb_ref, alog_ref, dsk_ref, ng_ref,
                a_ref, ncs_ref, nss_ref, e_ref, act_ref, y_ref, s_ref, acum_t_ref, dt_t_ref,
                *, taps, inner, groups, hpg, hd, t_valid):
    ci = pl.program_id(1)
    L = z_ref.shape[0]
    n = SSD_STATE
    gw = hpg * hd
    off = SSD_HIST - (taps - 1)

    @pl.when(ci == 0)
    def _():
        e_ref[0:SSD_HIST, :] = cst_ref[...]
        s_ref[...] = sst_ref[...]

    e_ref[SSD_HIST:SSD_HIST + L, :] = xbc_ref[...]
    conv = jnp.broadcast_to(bc_ref[...], (L, bc_ref.shape[1]))
    for w in range(taps):
        conv = conv + e_ref[off + w:off + w + L, :] * wc_ref[w:w + 1, :]
    act_ref[...] = _silu(conv)

    rows = lax.broadcasted_iota(jnp.int32, (L, LANES), 0) + ci * L
    dt = jnp.where(rows < t_valid, _softplus(dt_ref[...] + dtb_ref[...]), 0.0)
    da = dt * (-jnp.exp(alog_ref[...]))
    r = lax.broadcasted_iota(jnp.int32, (L, L), 0)
    c = lax.broadcasted_iota(jnp.int32, (L, L), 1)
    causal = r >= c
    tril = jnp.where(causal, 1.0, 0.0).astype(BF16)
    acum = sum(_dot(tril, part) for part in _split_bf16(da, 3))
    acum_t_ref[...] = acum.T
    dt_t_ref[...] = dt.T
    lane = lax.broadcasted_iota(jnp.int32, (L, LANES), 1)
    sub = lax.broadcasted_iota(jnp.int32, (LANES, n), 0)
    first = lane < hd
    first_rows = sub < hd

    def group(g, _):
        b_g = act_ref[:, pl.ds(pl.multiple_of(inner + g * n, n), n)].astype(BF16)
        c_g = act_ref[:, pl.ds(pl.multiple_of(inner + groups * n + g * n, n), n)].astype(BF16)
        cb = _dot_nt(c_g, b_g)
        for i in range(hpg // 2):
            h0 = g * hpg + 2 * i
            xcols = pl.ds(pl.multiple_of(g * gw + i * LANES, LANES), LANES)
            xp = act_ref[:, xcols]
            y = dsk_ref[:, xcols] * xp
            cols, rws, lasts = [], [], []
            for j in range(2):
                hh = h0 + j
                a_col = jnp.sum(jnp.where(lane == hh, acum, 0.0), axis=1, keepdims=True)
                a_row = acum_t_ref[pl.ds(hh, 1), :]
                dt_row = dt_t_ref[pl.ds(hh, 1), :]
                dec = jnp.exp(jnp.where(causal, a_col - a_row, -jnp.inf))
                sc = (cb * dec * dt_row).astype(BF16)
                xm = jnp.where(first if j == 0 else ~first, xp, 0.0).astype(BF16)
                y = y + _dot(sc, xm)
                last = a_row[:, L - 1:L]
                cols.append(a_col)
                rws.append(dt_row * jnp.exp(last - a_row))
                lasts.append(jnp.exp(last))
            srows = pl.ds(pl.multiple_of(h0 * hd, LANES), LANES)
            st = s_ref[srows, :]
            y = y + _dot_nt(c_g, st.astype(BF16)) * jnp.exp(jnp.where(first, cols[0], cols[1]))
            y_ref[:, xcols] = y
            xw_t = xp.T * jnp.where(first_rows, rws[0], rws[1])
            s_ref[srows, :] = st * jnp.where(first_rows, lasts[0], lasts[1]) + _dot(xw_t.astype(BF16), b_g)
        return 0

    lax.fori_loop(0, groups, group, 0)

    yz = y_ref[...] * _silu(z_ref[...])
    ms = jnp.mean(yz * yz, axis=-1, keepdims=True)
    a_ref[...] = (yz * lax.rsqrt(ms + EPS) * ng_ref[...]).astype(a_ref.dtype)

    @pl.when(ci == pl.num_programs(1) - 1)
    def _():
        tv = t_valid - (-(-t_valid // L) - 1) * L
        ncs_ref[...] = e_ref[SSD_HIST + tv - (taps - 1):SSD_HIST + tv, :]
        nss_ref[...] = s_ref[...]

    e_ref[0:SSD_HIST, :] = e_ref[L:L + SSD_HIST, :]


def ssd_module(z, xbc, dt, conv_state, ssm_state, w_conv, b_conv, dt_bias, a_log, d_skip, norm_g, batch, t_valid):
    nrow, inner = z.shape
    conv_dim = xbc.shape[1]
    taps = w_conv.shape[0]
    heads = dt_bias.shape[0]
    hd = inner // heads
    groups = (conv_dim - inner) // (2 * SSD_STATE)
    hpg = heads // groups
    L = SSD_CHUNK
    nc = nrow // batch // L
    assert 2 * hd == LANES and hpg % 2 == 0 and heads <= LANES and L == LANES
    padh = lambda v: jnp.pad(v, (0, LANES - heads)).reshape(1, LANES)
    cst = jnp.pad(conv_state, ((0, 0), (SSD_HIST - (taps - 1), 0), (0, 0)))
    sst = ssm_state.reshape(batch, heads * hd, SSD_STATE)
    row = lambda w: pl.BlockSpec((L, w), lambda b, i: (b * nc + i, 0))
    per_b = lambda shape: pl.BlockSpec((None,) + shape, lambda b, i: (b, 0, 0))
    return pl.pallas_call(
        functools.partial(_ssd_kernel, taps=taps, inner=inner, groups=groups, hpg=hpg, hd=hd, t_valid=t_valid),
        grid=(batch, nc),
        in_specs=[row(inner), row(conv_dim), row(LANES), per_b((SSD_HIST, conv_dim)), per_b((heads * hd, SSD_STATE)),
                  _const_spec((taps, conv_dim)), _const_spec((1, conv_dim)), _const_spec((1, LANES)),
                  _const_spec((1, LANES)), _const_spec((1, inner)), _const_spec((1, inner))],
        out_specs=[row(inner), per_b((taps - 1, conv_dim)), per_b((heads * hd, SSD_STATE))],
        out_shape=[jax.ShapeDtypeStruct((nrow, inner), BF16),
                   jax.ShapeDtypeStruct((batch, taps - 1, conv_dim), F32),
                   jax.ShapeDtypeStruct((batch, heads * hd, SSD_STATE), F32)],
        scratch_shapes=[pltpu.VMEM((SSD_HIST + L, conv_dim), F32), pltpu.VMEM((L, conv_dim), F32),
                        pltpu.VMEM((L, inner), F32), pltpu.VMEM((heads * hd, SSD_STATE), F32),
                        pltpu.VMEM((LANES, L), F32), pltpu.VMEM((LANES, L), F32)],
        compiler_params=_params("parallel", "arbitrary", vmem=VMEM_LIMIT),
        name="ssd_module",
    )(z, xbc, dt, cst, sst, w_conv, b_conv.reshape(1, conv_dim), padh(dt_bias), padh(a_log),
      jnp.repeat(d_skip, hd).reshape(1, inner), norm_g.reshape(1, inner))


def _tile(m, pref):
    return pref if m % pref == 0 else m


def kernel(x_prompt, x_sample, cache_k, cache_v, state_conv, state_ssm_conv, state_ssm, page_table, p_prompt, p_sample, norm_pre, norm_post, w_ple_gate, w_ple_proj, w_sb_in, w_sb_out, sb_bias, w_cv_in, w_cv_dw, b_cv_dw, ln_cv_g, ln_cv_b, w_cv_out, w_ssd_in, w_ssd_conv, b_ssd_conv, dt_bias, a_log, d_skip, norm_ssd, w_ssd_out):
    bp, tp, d = x_prompt.shape
    bs, ts, _ = x_sample.shape
    depth = norm_pre.shape[0]
    heads = sb_bias.shape[1]
    sb_width = w_sb_out.shape[1]
    dh = sb_width // heads
    page = cache_k.shape[2]
    cv_width = w_cv_out.shape[1]
    inner = norm_ssd.shape[1]
    conv_dim = w_ssd_conv.shape[2]
    ssd_heads = dt_bias.shape[1]
    L = SSD_CHUNK

    np_, ns_ = bp * tp, bs * ts
    xp = x_prompt.reshape(np_, d)
    xs = x_sample.reshape(ns_, d)
    tmp = _tile(np_, 512)
    tpp = _tile(np_, 256)
    bf = lambda w: w.astype(BF16)

    def project(h, w, tm, dtypes=(F32,)):
        return matmul(h, w, tm, _tile(w.shape[1], 2048), dtypes)

    k_pr, v_pr, k_sa, v_sa, cv_pr, cv_sa, sc_pr, sc_sa, ss_pr, ss_sa = ([] for _ in range(10))
    for i in range(depth):
        kind, j = i % N_MIXERS, i // N_MIXERS
        hp = rmsnorm_bf16(xp, norm_pre[i], tmp)
        hs = rmsnorm_bf16(xs, norm_pre[i], ns_)
        if kind == 0:
            w_in = [bf(w_sb_in[j][:, n * sb_width:(n + 1) * sb_width]) for n in range(4)]
            qp, gp = project(hp, w_in[0], tmp), project(hp, w_in[3], tmp)
            (kp, kp16), (vp, vp16) = (project(hp, w, tmp, (F32, BF16)) for w in w_in[1:3])
            tq = _tile(tp, 512)
            ap = sb_attention_prompt(qp, kp16, vp16, gp, sb_bias[j], bp, heads, tq, _tile(tq, 256))
            qs, ks, vs, gs = (project(hs, w, ns_).reshape(bs, ts, sb_width) for w in w_in)
            as_ = sb_attention_sample(qs, ks, vs, gs, sb_bias[j], cache_k, cache_v, j, page_table,
                                      math.gcd(page_table.shape[1], 4))
            as_ = as_.reshape(ns_, sb_width)
            w_out = bf(w_sb_out[j])
            k_pr.append(kp.reshape(bp, tp, heads, dh)); v_pr.append(vp.reshape(bp, tp, heads, dh))
            k_sa.append(ks.reshape(bs, ts, heads, dh)); v_sa.append(vs.reshape(bs, ts, heads, dh))
        elif kind == 1:
            w_in = [bf(w_cv_in[j][:, n * cv_width:(n + 1) * cv_width]) for n in range(3)]
            taps = w_cv_dw.shape[1]
            valp, gatep, zp = (project(hp, w, tmp) for w in w_in)
            zero_cv = jnp.zeros((bp, taps - 1, cv_width), F32)
            ap, cp = conv_module(valp, gatep, zp, zero_cv, w_cv_dw[j], b_cv_dw[j], ln_cv_g[j], ln_cv_b[j],
                                 bp, _tile(tp, 256), BF16)
            vals, gates, zs = (project(hs, w, ns_) for w in w_in)
            as_, cs = conv_module(vals, gates, zs, state_conv[j], w_cv_dw[j], b_cv_dw[j], ln_cv_g[j], ln_cv_b[j],
                                  bs, ts, F32)
            w_out = bf(w_cv_out[j])
            cv_pr.append(cp); cv_sa.append(cs)
        else:
            w = w_ssd_in[j]
            w_z = bf(w[:, :inner])
            w_xbc = bf(w[:, inner:inner + conv_dim])
            w_dt = bf(jnp.pad(w[:, inner + conv_dim:], ((0, 0), (0, LANES - ssd_heads))))
            taps = w_ssd_conv.shape[1]
            args = (w_ssd_conv[j], b_ssd_conv[j], dt_bias[j], a_log[j], d_skip[j], norm_ssd[j])
            zp, xbcp, dtp = (project(hp, ww, tmp) for ww in (w_z, w_xbc, w_dt))
            zero_sc = jnp.zeros((bp, taps - 1, conv_dim), F32)
            zero_ss = jnp.zeros((bp, ssd_heads, inner // ssd_heads, SSD_STATE), F32)
            ap, scp, ssp = ssd_module(zp, xbcp, dtp, zero_sc, zero_ss, *args, bp, tp)
            padrows = lambda u: jnp.pad(u.reshape(bs, ts, -1), ((0, 0), (0, L - ts), (0, 0))).reshape(bs * L, -1)
            zs, xbcs, dts = (padrows(project(hs, ww, ns_)) for ww in (w_z, w_xbc, w_dt))
            as_, scs, sss = ssd_module(zs, xbcs, dts, state_ssm_conv[j], state_ssm[j], *args, bs, ts)
            as_ = as_.reshape(bs, L, inner)[:, :ts].reshape(ns_, inner)
            w_out = bf(w_ssd_out[j])
            shape5 = (-1, ssd_heads, inner // ssd_heads, SSD_STATE)
            sc_pr.append(scp); sc_sa.append(scs); ss_pr.append(ssp.reshape(shape5)); ss_sa.append(sss.reshape(shape5))
        w_gate, w_proj = bf(w_ple_gate[i]), bf(w_ple_proj[i])
        xp = proj_residual_ple(ap, w_out, xp, norm_post[i], w_gate, p_prompt[i].reshape(np_, -1), w_proj, tpp)
        xs = proj_residual_ple(as_, w_out, xs, norm_post[i], w_gate, p_sample[i].reshape(ns_, -1), w_proj, ns_)
    return (xp.reshape(bp, tp, d), xs.reshape(bs, ts, d), jnp.stack(k_pr), jnp.stack(v_pr), jnp.stack(k_sa),
            jnp.stack(v_sa), jnp.stack(cv_pr), jnp.stack(cv_sa), jnp.stack(sc_pr), jnp.stack(sc_sa),
            jnp.stack(ss_pr), jnp.stack(ss_sa))
```

```python
import functools
import math

import jax
import jax.numpy as jnp
from jax import lax
from jax.experimental import pallas as pl
from jax.experimental.pallas import tpu as pltpu

F32 = jnp.float32
BF16 = jnp.bfloat16
EPS = 1e-6
LOG2E = math.log2(math.e)
N_MIXERS = 3
SSD_STATE = 128
SSD_CHUNK = 128
LANES = 128
SUBLANES = 8
VMEM_LIMIT = 56 * 1024 * 1024


def _params(*sem, vmem=None):
    return pltpu.CompilerParams(dimension_semantics=sem, vmem_limit_bytes=vmem)


def _softplus(z):
    return jnp.maximum(z, 0.0) + jnp.log(1.0 + jnp.exp(-jnp.abs(z)))


def _neg_abs(z):
    sign = jnp.uint32(0x80000000)
    return lax.bitcast_convert_type(lax.bitcast_convert_type(z, jnp.uint32) | sign, F32)


def _softplus2(z2):
    return jnp.maximum(z2, 0.0) + jnp.log(1.0 + jnp.exp2(_neg_abs(z2))) * LOG2E


def _sigmoid(z):
    return 1.0 / (1.0 + jnp.exp(-z))


def _silu(z):
    return z * _sigmoid(z)


def _dot(a, b):
    return jnp.dot(a, b, preferred_element_type=F32)


def _dot_nt(a, b):
    return lax.dot_general(a, b, (((1,), (1,)), ((), ())), preferred_element_type=F32)


def _split_bf16(x, terms):
    parts = []
    r = x
    for t in range(terms):
        p = r.astype(BF16)
        parts.append(p)
        if t + 1 < terms:
            r = r - p.astype(F32)
    return parts


def _const_spec(shape, single_buffer=False):
    zeros = (0,) * len(shape)
    if single_buffer:
        return pl.BlockSpec(shape, lambda *_: zeros, pipeline_mode=pl.Buffered(1))
    return pl.BlockSpec(shape, lambda *_: zeros)


def _rmsnorm_kernel(x_ref, g_ref, o_ref):
    x = x_ref[...]
    ms = jnp.mean(x * x, axis=-1, keepdims=True)
    o_ref[...] = (x * lax.rsqrt(ms + EPS) * g_ref[...]).astype(o_ref.dtype)


def rmsnorm_bf16(x, g, tm):
    m, d = x.shape
    return pl.pallas_call(
        _rmsnorm_kernel,
        grid=(m // tm,),
        in_specs=[pl.BlockSpec((tm, d), lambda i: (i, 0)), _const_spec((1, d))],
        out_specs=pl.BlockSpec((tm, d), lambda i: (i, 0)),
        out_shape=jax.ShapeDtypeStruct((m, d), BF16),
        compiler_params=_params("parallel"),
        name="rmsnorm",
    )(x, g.reshape(1, d))


def _matmul_kernel(a_ref, w_ref, *o_refs):
    r = _dot(a_ref[...], w_ref[...])
    for o_ref in o_refs:
        o_ref[...] = r.astype(o_ref.dtype)


def matmul(a, w, tm, tn, dtypes=(F32,)):
    m, k = a.shape
    n = w.shape[1]
    out = pl.pallas_call(
        _matmul_kernel,
        grid=(n // tn, m // tm),
        in_specs=[pl.BlockSpec((tm, k), lambda j, i: (i, 0)), pl.BlockSpec((k, tn), lambda j, i: (0, j))],
        out_specs=[pl.BlockSpec((tm, tn), lambda j, i: (i, j)) for _ in dtypes],
        out_shape=[jax.ShapeDtypeStruct((m, n), dt) for dt in dtypes],
        compiler_params=_params("parallel", "parallel", vmem=VMEM_LIMIT),
        name="matmul",
    )(a, w)
    return out[0] if len(dtypes) == 1 else out


def _proj_kernel(a_ref, wo_ref, x_ref, g_ref, wg_ref, p_ref, wp_ref, o_ref):
    m = _dot(a_ref[...].astype(BF16), wo_ref[...])
    ms = jnp.mean(m * m, axis=-1, keepdims=True)
    xn = x_ref[...] + m * lax.rsqrt(ms + EPS) * g_ref[...]
    gate = _sigmoid(_dot(xn.astype(BF16), wg_ref[...]))
    pe = _dot(p_ref[...].astype(BF16), wp_ref[...])
    o_ref[...] = xn + gate * pe


def proj_residual_ple(a, w_out, x, g_post, w_gate, p, w_proj, tm):
    m, k = a.shape
    d = x.shape[1]
    e = p.shape[1]
    row = lambda i: (i, 0)
    return pl.pallas_call(
        _proj_kernel,
        grid=(m // tm,),
        in_specs=[
            pl.BlockSpec((tm, k), row),
            _const_spec((k, d), single_buffer=True),
            pl.BlockSpec((tm, d), row),
            _const_spec((1, d)),
            _const_spec((d, d), single_buffer=True),
            pl.BlockSpec((tm, e), row),
            _const_spec((e, d), single_buffer=True),
        ],
        out_specs=pl.BlockSpec((tm, d), row),
        out_shape=jax.ShapeDtypeStruct((m, d), F32),
        compiler_params=_params("parallel", vmem=VMEM_LIMIT),
        name="proj_residual_ple",
    )(a, w_out, x, g_post.reshape(1, d), w_gate, p, w_proj)


def _sb_scores(z2, trib, mask):
    tk = z2.shape[1]
    sp = _softplus2(z2)
    drop = sp if mask is None else jnp.where(mask, sp, 0.0)
    cs = _dot(drop.astype(BF16), trib)
    if trib.shape[1] == tk:
        total = jnp.broadcast_to(-jnp.sum(drop, axis=1, keepdims=True), (z2.shape[0], LANES))
        return z2 - sp + cs, total
    return z2 - sp + cs[:, :tk], cs[:, tk:]


def _tri_ones(t, total_width):
    r = lax.broadcasted_iota(jnp.int32, (t, t + total_width), 0)
    c = lax.broadcasted_iota(jnp.int32, (t, t + total_width), 1)
    return jnp.where((r > c) | (c >= t), -1.0, 0.0).astype(BF16)


def _sb_prompt_kernel(bias_ref, q_ref, k_ref, v_ref, g_ref, o_ref, z_ref, w_ref, c_ref, acc_ref, *, scale, tq, tk):
    h = pl.program_id(1)
    qi = pl.program_id(2)
    bias = bias_ref[h] * LOG2E
    nd = tq // tk
    trib = _tri_ones(tk, 0)
    q = (q_ref[...] * (scale * LOG2E)).astype(BF16)
    c_ref[...] = jnp.zeros_like(c_ref)
    acc_ref[...] = jnp.zeros_like(acc_ref)

    n = (qi + 1) * nd

    def kv(ref, i):
        kb = jnp.maximum(n - 1 - i, 0)
        return ref[pl.ds(pl.multiple_of(kb * tk, tk), tk), :]

    def logits(i):
        z_ref[...] = _dot_nt(q, kv(k_ref, i)) + bias

    def weights(i, masked):
        mask = None
        if masked:
            row = lax.broadcasted_iota(jnp.int32, (tq, tk), 0)
            col = lax.broadcasted_iota(jnp.int32, (tq, tk), 1)
            mask = col + (nd - 1 - i) * tk < row
        logw, total = _sb_scores(z_ref[...], trib, mask)
        c = c_ref[...]
        w = jnp.exp2(logw + jnp.concatenate([c] * (tk // c.shape[1]), axis=1))
        w_ref[...] = (w if mask is None else jnp.where(mask, w, 0.0)).astype(BF16)
        c_ref[...] = c + total

    def accumulate(i):
        acc_ref[...] += _dot(w_ref[...], kv(v_ref, i))

    logits(0)
    weights(0, True)
    logits(1)
    for i in range(nd - 1):
        accumulate(i)
        weights(i + 1, True)
        logits(i + 2)

    def body(i, _):
        accumulate(i)
        weights(i + 1, False)
        logits(i + 2)
        return 0

    lax.fori_loop(nd - 1, n, body, 0)
    o_ref[...] = (acc_ref[...] * _silu(g_ref[...])).astype(o_ref.dtype)


def sb_attention_prompt(q, k, v, g, bias, batch, heads, tq, tk):
    n, width = q.shape
    t = n // batch
    dh = width // heads
    nq = t // tq
    assert tq % tk == 0 and tk % LANES == 0
    qspec = pl.BlockSpec((tq, dh), lambda b, h, i: (b * nq + i, h))
    kvspec = pl.BlockSpec((t, dh), lambda b, h, i: (b, h))
    return pl.pallas_call(
        functools.partial(_sb_prompt_kernel, scale=1.0 / math.sqrt(dh), tq=tq, tk=tk),
        grid=(batch, heads, nq),
        in_specs=[pl.BlockSpec(memory_space=pltpu.SMEM), qspec, kvspec, kvspec, qspec],
        out_specs=qspec,
        out_shape=jax.ShapeDtypeStruct((n, width), BF16),
        scratch_shapes=[pltpu.VMEM((tq, tk), F32), pltpu.VMEM((tq, tk), BF16), pltpu.VMEM((tq, LANES), F32),
                        pltpu.VMEM((tq, dh), F32)],
        compiler_params=_params("parallel", "parallel", "parallel"),
        name="sb_attention_prompt",
    )(bias, q, k, v, g)


def _sb_sample_kernel(pt_ref, qbd_ref, bias_ref, kn_ref, vn_ref, *refs, heads, tq, pages_per_step):
    kc_refs = refs[:pages_per_step]
    vc_refs = refs[pages_per_step:2 * pages_per_step]
    g_ref, o_ref, c_ref, acc_ref = refs[2 * pages_per_step:]
    s = pl.program_id(1)
    page = kn_ref.shape[0] // heads
    dh = kn_ref.shape[1]
    hq = heads * tq
    half = heads // 2
    rows = 2 * page
    ri = lax.broadcasted_iota(jnp.int32, (rows, hq), 0)
    li = lax.broadcasted_iota(jnp.int32, (rows, hq), 1)
    valid = ri % 2 == li // (tq * half)
    r2 = lax.broadcasted_iota(jnp.int32, (rows + SUBLANES, rows), 0)
    c2 = lax.broadcasted_iota(jnp.int32, (rows + SUBLANES, rows), 1)
    trib = jnp.where(((c2 // 2 > r2 // 2) & (c2 % 2 == r2 % 2)) | (r2 >= rows), -1.0, 0.0).astype(BF16)

    def pair_rows(ref, p):
        return ref[pl.ds(p, rows, stride=half), :].astype(BF16)

    def tile(k_ref, v_ref, mask, c, acc):
        k_all = jnp.concatenate([pair_rows(k_ref, p) for p in range(half)], axis=1)
        z2 = bias_ref[...] + _dot(k_all, qbd_ref[...])
        sp = _softplus2(z2)
        keep = valid if mask is None else valid & mask
        cs = _dot(trib, jnp.where(keep, sp, 0.0).astype(BF16))
        w = jnp.where(keep, jnp.exp2(z2 - sp + cs[:rows] + c[0:1, :]), 0.0)
        wt = w.T
        out = [None] * heads
        for p in range(half):
            q = p + half
            lhs = jnp.concatenate([wt[p * tq:(p + 1) * tq], wt[q * tq:(q + 1) * tq]], axis=0).astype(BF16)
            res = _dot(lhs, pair_rows(v_ref, p))
            out[p], out[q] = res[:tq], res[tq:]
        return c + cs[rows:], acc + jnp.concatenate(out, axis=1)

    @pl.when(s == 0)
    def _():
        zero = jnp.zeros(acc_ref.shape, F32)
        c_ref[...], acc_ref[...] = tile(kn_ref, vn_ref, ri // 2 < li % tq, zero[:, :hq], zero)

    @pl.when(s > 0)
    def _():
        c, acc = c_ref[...], acc_ref[...]
        for k_ref, v_ref in zip(kc_refs, vc_refs):
            c, acc = tile(k_ref, v_ref, None, c, acc)
        c_ref[...], acc_ref[...] = c, acc

    @pl.when(s == pl.num_programs(1) - 1)
    def _():
        o_ref[...] = acc_ref[...] * _silu(g_ref[...])


def sb_attention_sample(q, k, v, g, bias, cache_k, cache_v, layer, page_table, pages_per_step):
    seqs, tq, width = q.shape
    _, pool, page, heads, dh = cache_k.shape
    hq = heads * tq
    n_pages = page_table.shape[1]
    assert hq == LANES and tq == SUBLANES and tq <= page and n_pages % pages_per_step == 0
    cache_k = cache_k.reshape(-1, pool, page * heads, dh)
    cache_v = cache_v.reshape(-1, pool, page * heads, dh)
    scale = LOG2E / math.sqrt(dh)
    q4 = (q * scale).reshape(seqs, tq, heads, dh)
    eye = jnp.eye(heads, dtype=F32)
    qbd = jnp.einsum("bqhd,hg->bhdgq", q4, eye).reshape(seqs, 2, width // 2, hq)
    qbd = (qbd[:, 0] + qbd[:, 1]).astype(BF16)
    bias_row = jnp.repeat(bias * LOG2E, tq).reshape(1, hq)
    pad = lambda u: jnp.pad(u, ((0, 0), (0, page - tq), (0, 0))).reshape(seqs, page * heads, dh)
    seq_spec = lambda shape: pl.BlockSpec((None,) + shape, lambda b, s, pt: (b, 0, 0))

    def cache_spec(p):
        def index(b, s, pt):
            return layer, pt[b, n_pages - 1 - ((jnp.maximum(s, 1) - 1) * pages_per_step + p)], 0, 0
        return pl.BlockSpec((None, None, page * heads, dh), index)

    cache_specs = [cache_spec(p) for p in range(pages_per_step)]
    grid_spec = pltpu.PrefetchScalarGridSpec(
        num_scalar_prefetch=1,
        grid=(seqs, n_pages // pages_per_step + 1),
        in_specs=[seq_spec((width // 2, hq)), pl.BlockSpec((1, hq), lambda b, s, pt: (0, 0)),
                  seq_spec((page * heads, dh)), seq_spec((page * heads, dh)), *cache_specs, *cache_specs,
                  seq_spec((tq, width))],
        out_specs=seq_spec((tq, width)),
        scratch_shapes=[pltpu.VMEM((SUBLANES, hq), F32), pltpu.VMEM((tq, width), F32)],
    )
    return pl.pallas_call(
        functools.partial(_sb_sample_kernel, heads=heads, tq=tq, pages_per_step=pages_per_step),
        grid_spec=grid_spec,
        out_shape=jax.ShapeDtypeStruct((seqs, tq, width), F32),
        compiler_params=_params("parallel", "arbitrary", vmem=VMEM_LIMIT),
        name="sb_attention_sample",
    )(page_table, qbd, bias_row, pad(k), pad(v), *([cache_k] * pages_per_step), *([cache_v] * pages_per_step), g)


CV_HIST = 32


def _conv_kernel(val_ref, gate_ref, z_ref, st_ref, w_ref, b_ref, lg_ref, lb_ref, a_ref, ns_ref, e_ref, co_ref,
                 win_ref, *, taps, tt):
    ti = pl.program_id(1)
    off = CV_HIST - (taps - 1)

    @pl.when(ti == 0)
    def _():
        e_ref[0:CV_HIST, :] = st_ref[...]

    e_ref[CV_HIST:CV_HIST + tt, :] = val_ref[...] * _sigmoid(gate_ref[...])
    rb = min(tt, LANES)

    def column_block(ci, _):
        cols = pl.ds(pl.multiple_of(ci * LANES, LANES), LANES)
        for r0 in range(0, tt, rb):
            acc = jnp.broadcast_to(b_ref[:, cols], (rb, LANES))
            for s in range(SUBLANES):
                ws = [w for w in range(taps) if (off + w) % SUBLANES == s]
                if not ws:
                    continue
                span = rb + ws[-1] - ws[0]
                win_ref[0:span, :] = e_ref[pl.ds(off + ws[0] + r0, span), cols]
                for w in ws:
                    acc = acc + win_ref[w - ws[0]:w - ws[0] + rb, :] * w_ref[pl.ds(w, 1), cols]
            co_ref[pl.ds(r0, rb), cols] = acc
        return 0

    lax.fori_loop(0, e_ref.shape[1] // LANES, column_block, 0)
    acc = co_ref[...]
    mu = jnp.mean(acc, axis=-1, keepdims=True)
    cen = acc - mu
    var = jnp.mean(cen * cen, axis=-1, keepdims=True)
    c = _silu(cen * lax.rsqrt(var + EPS) * lg_ref[...] + lb_ref[...])
    a_ref[...] = (c * _silu(z_ref[...])).astype(a_ref.dtype)

    @pl.when(ti == pl.num_programs(1) - 1)
    def _():
        ns_ref[...] = e_ref[tt + off:tt + CV_HIST, :]

    e_ref[0:CV_HIST, :] = e_ref[tt:tt + CV_HIST, :]


def conv_module(val, gate, z, state, w_dw, b_dw, ln_g, ln_b, batch, tt, out_dtype):
    n, c = val.shape
    taps = w_dw.shape[0]
    nt = n // batch // tt
    st = jnp.pad(state, ((0, 0), (CV_HIST - (taps - 1), 0), (0, 0)))
    row = pl.BlockSpec((tt, c), lambda b, i: (b * nt + i, 0))
    vec = _const_spec((1, c))
    return pl.pallas_call(
        functools.partial(_conv_kernel, taps=taps, tt=tt),
        grid=(batch, nt),
        in_specs=[row, row, row, pl.BlockSpec((None, CV_HIST, c), lambda b, i: (b, 0, 0)),
                  _const_spec((taps, c)), vec, vec, vec],
        out_specs=[row, pl.BlockSpec((None, taps - 1, c), lambda b, i: (b, 0, 0))],
        out_shape=[jax.ShapeDtypeStruct((n, c), out_dtype), jax.ShapeDtypeStruct((batch, taps - 1, c), F32)],
        scratch_shapes=[pltpu.VMEM((CV_HIST + tt, c), F32), pltpu.VMEM((tt, c), F32),
                        pltpu.VMEM((min(tt, LANES) + CV_HIST, LANES), F32)],
        compiler_params=_params("parallel", "arbitrary", vmem=VMEM_LIMIT),
        name="conv_module",
    )(val, gate, z, st, w_dw, b_dw.reshape(1, c), ln_g.reshape(1, c), ln_b.reshape(1, c))


SSD_HIST = 8


def _ssd_kernel(z_ref, xbc_ref, dt_ref, cst_ref, sst_ref, wc_ref, bc_ref, dtb_ref, alog_ref, dsk_ref, ng_ref,
                a_ref, ncs_ref, nss_ref, e_ref, act_ref, y_ref, s_ref, acum_t_ref, dt_t_ref,
                *, taps, inner, groups, hpg, hd, t_valid):
    ci = pl.program_id(1)
    L = SSD_CHUNK
    rin = z_ref.shape[0]
    n = SSD_STATE
    gw = hpg * hd
    off = SSD_HIST - (taps - 1)
    cw = 2 * LANES

    @pl.when(ci == 0)
    def _():
        e_ref[0:SSD_HIST, :] = cst_ref[...]
        s_ref[...] = sst_ref[...]

    e_ref[SSD_HIST:SSD_HIST + rin, :] = xbc_ref[...]
    dt_raw = dt_ref[...]
    if rin < L:
        e_ref[SSD_HIST + rin:SSD_HIST + L, :] = jnp.zeros((L - rin, e_ref.shape[1]), F32)
        dt_raw = jnp.concatenate([dt_raw, jnp.zeros((L - rin, LANES), F32)], axis=0)

    def conv_block(bi, _):
        cols = pl.ds(pl.multiple_of(bi * cw, cw), cw)
        conv = jnp.broadcast_to(bc_ref[:, cols], (L, cw))
        for w in range(taps):
            conv = conv + e_ref[pl.ds(off + w, L), cols] * wc_ref[pl.ds(w, 1), cols]
        act_ref[:, cols] = _silu(conv)
        return 0

    lax.fori_loop(0, e_ref.shape[1] // cw, conv_block, 0)

    rows = lax.broadcasted_iota(jnp.int32, (L, LANES), 0) + ci * L
    dt = jnp.where(rows < t_valid, _softplus(dt_raw + dtb_ref[...]), 0.0)
    da = dt * (-jnp.exp(alog_ref[...]))
    r = lax.broadcasted_iota(jnp.int32, (L, L), 0)
    c = lax.broadcasted_iota(jnp.int32, (L, L), 1)
    causal = r >= c
    tril = jnp.where(causal, 1.0, 0.0).astype(BF16)
    acum = sum(_dot(tril, part) for part in _split_bf16(da, 3))
    acum_t_ref[...] = acum.T
    dt_t_ref[...] = dt.T
    lane = lax.broadcasted_iota(jnp.int32, (L, LANES), 1)
    sub = lax.broadcasted_iota(jnp.int32, (LANES, n), 0)
    first = lane < hd
    first_rows = sub < hd

    def group(g, _):
        b_g = act_ref[:, pl.ds(pl.multiple_of(inner + g * n, n), n)].astype(BF16)
        c_g = act_ref[:, pl.ds(pl.multiple_of(inner + groups * n + g * n, n), n)].astype(BF16)
        cb = _dot_nt(c_g, b_g)
        for i in range(hpg // 2):
            h0 = g * hpg + 2 * i
            xcols = pl.ds(pl.multiple_of(g * gw + i * LANES, LANES), LANES)
            xp = act_ref[:, xcols]
            y = dsk_ref[:, xcols] * xp
            cols, rws, lasts = [], [], []
            for j in range(2):
                hh = h0 + j
                a_col = jnp.sum(jnp.where(lane == hh, acum, 0.0), axis=1, keepdims=True)
                a_row = acum_t_ref[pl.ds(hh, 1), :]
                dt_row = dt_t_ref[pl.ds(hh, 1), :]
                dec = jnp.exp(jnp.where(causal, a_col - a_row, -jnp.inf))
                sc = (cb * dec * dt_row).astype(BF16)
                xm = jnp.where(first if j == 0 else ~first, xp, 0.0).astype(BF16)
                y = y + _dot(sc, xm)
                last = a_row[:, L - 1:L]
                cols.append(a_col)
                rws.append(dt_row * jnp.exp(last - a_row))
                lasts.append(jnp.exp(last))
            srows = pl.ds(pl.multiple_of(h0 * hd, LANES), LANES)
            st = s_ref[srows, :]
            y = y + _dot_nt(c_g, st.astype(BF16)) * jnp.exp(jnp.where(first, cols[0], cols[1]))
            y_ref[:, xcols] = y
            xw_t = xp.T * jnp.where(first_rows, rws[0], rws[1])
            s_ref[srows, :] = st * jnp.where(first_rows, lasts[0], lasts[1]) + _dot(xw_t.astype(BF16), b_g)
        return 0

    lax.fori_loop(0, groups, group, 0)

    def gate_block(bi, ss):
        cols = pl.ds(pl.multiple_of(bi * cw, cw), cw)
        yz = y_ref[0:rin, cols] * _silu(z_ref[:, cols])
        y_ref[0:rin, cols] = yz
        return ss + jnp.sum(yz * yz, axis=1, keepdims=True)

    ss = lax.fori_loop(0, inner // cw, gate_block, jnp.zeros((rin, 1), F32))
    a_ref[...] = (y_ref[0:rin, :] * lax.rsqrt(ss / inner + EPS) * ng_ref[...]).astype(a_ref.dtype)

    @pl.when(ci == pl.num_programs(1) - 1)
    def _():
        tv = t_valid - (-(-t_valid // L) - 1) * L
        ncs_ref[...] = e_ref[SSD_HIST + tv - (taps - 1):SSD_HIST + tv, :]
        nss_ref[...] = s_ref[...]

    e_ref[0:SSD_HIST, :] = e_ref[L:L + SSD_HIST, :]


def ssd_module(z, xbc, dt, conv_state, ssm_state, w_conv, b_conv, dt_bias, a_log, d_skip, norm_g, batch):
    nrow, inner = z.shape
    conv_dim = xbc.shape[1]
    taps = w_conv.shape[0]
    heads = dt_bias.shape[0]
    hd = inner // heads
    groups = (conv_dim - inner) // (2 * SSD_STATE)
    hpg = heads // groups
    L = SSD_CHUNK
    t_valid = nrow // batch
    rin = min(L, t_valid)
    nc = t_valid // rin
    assert 2 * hd == LANES and hpg % 2 == 0 and heads <= LANES and L == LANES
    assert t_valid % rin == 0 and rin % SUBLANES == 0 and taps - 1 <= rin
    padh = lambda v: jnp.pad(v, (0, LANES - heads)).reshape(1, LANES)
    cst = jnp.pad(conv_state, ((0, 0), (SSD_HIST - (taps - 1), 0), (0, 0)))
    sst = ssm_state.reshape(batch, heads * hd, SSD_STATE)
    row = lambda w: pl.BlockSpec((rin, w), lambda b, i: (b * nc + i, 0))
    per_b = lambda shape: pl.BlockSpec((None,) + shape, lambda b, i: (b, 0, 0))
    return pl.pallas_call(
        functools.partial(_ssd_kernel, taps=taps, inner=inner, groups=groups, hpg=hpg, hd=hd, t_valid=t_valid),
        grid=(batch, nc),
        in_specs=[row(inner), row(conv_dim), row(LANES), per_b((SSD_HIST, conv_dim)), per_b((heads * hd, SSD_STATE)),
                  _const_spec((taps, conv_dim)), _const_spec((1, conv_dim)), _const_spec((1, LANES)),
                  _const_spec((1, LANES)), _const_spec((1, inner)), _const_spec((1, inner))],
        out_specs=[row(inner), per_b((taps - 1, conv_dim)), per_b((heads * hd, SSD_STATE))],
        out_shape=[jax.ShapeDtypeStruct((nrow, inner), BF16 if rin == L else F32),
                   jax.ShapeDtypeStruct((batch, taps - 1, conv_dim), F32),
                   jax.ShapeDtypeStruct((batch, heads * hd, SSD_STATE), F32)],
        scratch_shapes=[pltpu.VMEM((SSD_HIST + L, conv_dim), F32), pltpu.VMEM((L, conv_dim), F32),
                        pltpu.VMEM((L, inner), F32), pltpu.VMEM((heads * hd, SSD_STATE), F32),
                        pltpu.VMEM((LANES, L), F32), pltpu.VMEM((LANES, L), F32)],
        compiler_params=_params("parallel", "arbitrary", vmem=VMEM_LIMIT),
        name="ssd_module",
    )(z, xbc, dt, cst, sst, w_conv, b_conv.reshape(1, conv_dim), padh(dt_bias), padh(a_log),
      jnp.repeat(d_skip, hd).reshape(1, inner), norm_g.reshape(1, inner))


def _tile(m, pref):
    return pref if m % pref == 0 else m


def kernel(x_prompt, x_sample, cache_k, cache_v, state_conv, state_ssm_conv, state_ssm, page_table, p_prompt, p_sample, norm_pre, norm_post, w_ple_gate, w_ple_proj, w_sb_in, w_sb_out, sb_bias, w_cv_in, w_cv_dw, b_cv_dw, ln_cv_g, ln_cv_b, w_cv_out, w_ssd_in, w_ssd_conv, b_ssd_conv, dt_bias, a_log, d_skip, norm_ssd, w_ssd_out):
    bp, tp, d = x_prompt.shape
    bs, ts, _ = x_sample.shape
    depth = norm_pre.shape[0]
    heads = sb_bias.shape[1]
    sb_width = w_sb_out.shape[1]
    dh = sb_width // heads
    page = cache_k.shape[2]
    cv_width = w_cv_out.shape[1]
    inner = norm_ssd.shape[1]
    conv_dim = w_ssd_conv.shape[2]
    ssd_heads = dt_bias.shape[1]
    L = SSD_CHUNK

    np_, ns_ = bp * tp, bs * ts
    xp = x_prompt.reshape(np_, d)
    xs = x_sample.reshape(ns_, d)
    tmp = _tile(np_, 512)
    tpp = _tile(np_, 256)
    bf = lambda w: w.astype(BF16)

    def project(h, w, tm, dtypes=(F32,)):
        return matmul(h, w, tm, _tile(w.shape[1], 2048), dtypes)

    k_pr, v_pr, k_sa, v_sa, cv_pr, cv_sa, sc_pr, sc_sa, ss_pr, ss_sa = ([] for _ in range(10))
    for i in range(depth):
        kind, j = i % N_MIXERS, i // N_MIXERS
        hp = rmsnorm_bf16(xp, norm_pre[i], tmp)
        hs = rmsnorm_bf16(xs, norm_pre[i], ns_)
        if kind == 0:
            w_in = [bf(w_sb_in[j][:, n * sb_width:(n + 1) * sb_width]) for n in range(4)]
            qp, gp = project(hp, w_in[0], tmp), project(hp, w_in[3], tmp)
            (kp, kp16), (vp, vp16) = (project(hp, w, tmp, (F32, BF16)) for w in w_in[1:3])
            tq = _tile(tp, 512)
            ap = sb_attention_prompt(qp, kp16, vp16, gp, sb_bias[j], bp, heads, tq, _tile(tq, 256))
            qs, ks, vs, gs = (project(hs, w, ns_).reshape(bs, ts, sb_width) for w in w_in)
            as_ = sb_attention_sample(qs, ks, vs, gs, sb_bias[j], cache_k, cache_v, j, page_table,
                                      math.gcd(page_table.shape[1], 4))
            as_ = as_.reshape(ns_, sb_width)
            w_out = bf(w_sb_out[j])
            k_pr.append(kp.reshape(bp, tp, heads, dh)); v_pr.append(vp.reshape(bp, tp, heads, dh))
            k_sa.append(ks.reshape(bs, ts, heads, dh)); v_sa.append(vs.reshape(bs, ts, heads, dh))
        elif kind == 1:
            w_in = [bf(w_cv_in[j][:, n * cv_width:(n + 1) * cv_width]) for n in range(3)]
            taps = w_cv_dw.shape[1]
            valp, gatep, zp = (project(hp, w, tmp) for w in w_in)
            zero_cv = jnp.zeros((bp, taps - 1, cv_width), F32)
            ap, cp = conv_module(valp, gatep, zp, zero_cv, w_cv_dw[j], b_cv_dw[j], ln_cv_g[j], ln_cv_b[j],
                                 bp, _tile(tp, 256), BF16)
            vals, gates, zs = (project(hs, w, ns_) for w in w_in)
            as_, cs = conv_module(vals, gates, zs, state_conv[j], w_cv_dw[j], b_cv_dw[j], ln_cv_g[j], ln_cv_b[j],
                                  bs, ts, F32)
            w_out = bf(w_cv_out[j])
            cv_pr.append(cp); cv_sa.append(cs)
        else:
            w = w_ssd_in[j]
            w_z = bf(w[:, :inner])
            w_xbc = bf(w[:, inner:inner + conv_dim])
            w_dt = bf(jnp.pad(w[:, inner + conv_dim:], ((0, 0), (0, LANES - ssd_heads))))
            taps = w_ssd_conv.shape[1]
            args = (w_ssd_conv[j], b_ssd_conv[j], dt_bias[j], a_log[j], d_skip[j], norm_ssd[j])
            zp, xbcp, dtp = (project(hp, ww, tmp) for ww in (w_z, w_xbc, w_dt))
            zero_sc = jnp.zeros((bp, taps - 1, conv_dim), F32)
            zero_ss = jnp.zeros((bp, ssd_heads, inner // ssd_heads, SSD_STATE), F32)
            ap, scp, ssp = ssd_module(zp, xbcp, dtp, zero_sc, zero_ss, *args, bp)
            zs, xbcs, dts = (project(hs, ww, ns_) for ww in (w_z, w_xbc, w_dt))
            as_, scs, sss = ssd_module(zs, xbcs, dts, state_ssm_conv[j], state_ssm[j], *args, bs)
            w_out = bf(w_ssd_out[j])
            shape5 = (-1, ssd_heads, inner // ssd_heads, SSD_STATE)
            sc_pr.append(scp); sc_sa.append(scs); ss_pr.append(ssp.reshape(shape5)); ss_sa.append(sss.reshape(shape5))
        w_gate, w_proj = bf(w_ple_gate[i]), bf(w_ple_proj[i])
        xp = proj_residual_ple(ap, w_out, xp, norm_post[i], w_gate, p_prompt[i].reshape(np_, -1), w_proj, tpp)
        xs = proj_residual_ple(as_, w_out, xs, norm_post[i], w_gate, p_sample[i].reshape(ns_, -1), w_proj, ns_)
    return (xp.reshape(bp, tp, d), xs.reshape(bs, ts, d), jnp.stack(k_pr), jnp.stack(v_pr), jnp.stack(k_sa),
            jnp.stack(v_sa), jnp.stack(cv_pr), jnp.stack(cv_sa), jnp.stack(sc_pr), jnp.stack(sc_sa),
            jnp.stack(ss_pr), jnp.stack(ss_sa))
```

```python
import functools
import math

import jax
import jax.numpy as jnp
from jax import lax
from jax.experimental import pallas as pl
from jax.experimental.pallas import tpu as pltpu

F32 = jnp.float32
BF16 = jnp.bfloat16
EPS = 1e-6
LOG2E = math.log2(math.e)
N_MIXERS = 3
SSD_STATE = 128
SSD_CHUNK = 128
LANES = 128
SUBLANES = 8
VMEM_LIMIT = 56 * 1024 * 1024


def _params(*sem, vmem=None):
    return pltpu.CompilerParams(dimension_semantics=sem, vmem_limit_bytes=vmem)


def _softplus(z):
    return jnp.maximum(z, 0.0) + jnp.log(1.0 + jnp.exp(-jnp.abs(z)))


def _neg_abs(z):
    sign = jnp.uint32(0x80000000)
    return lax.bitcast_convert_type(lax.bitcast_convert_type(z, jnp.uint32) | sign, F32)


def _softplus2(z2):
    return jnp.maximum(z2, 0.0) + jnp.log(1.0 + jnp.exp2(_neg_abs(z2))) * LOG2E


def _sigmoid(z):
    return 1.0 / (1.0 + jnp.exp(-z))


def _silu(z):
    return z * _sigmoid(z)


def _dot(a, b):
    return jnp.dot(a, b, preferred_element_type=F32)


def _dot_nt(a, b):
    return lax.dot_general(a, b, (((1,), (1,)), ((), ())), preferred_element_type=F32)


def _split_bf16(x, terms):
    parts = []
    r = x
    for t in range(terms):
        p = r.astype(BF16)
        parts.append(p)
        if t + 1 < terms:
            r = r - p.astype(F32)
    return parts


def _const_spec(shape, single_buffer=False):
    zeros = (0,) * len(shape)
    if single_buffer:
        return pl.BlockSpec(shape, lambda *_: zeros, pipeline_mode=pl.Buffered(1))
    return pl.BlockSpec(shape, lambda *_: zeros)


def _rmsnorm_kernel(x_ref, g_ref, o_ref):
    x = x_ref[...]
    ms = jnp.mean(x * x, axis=-1, keepdims=True)
    o_ref[...] = (x * lax.rsqrt(ms + EPS) * g_ref[...]).astype(o_ref.dtype)


def rmsnorm_bf16(x, g, tm):
    m, d = x.shape
    return pl.pallas_call(
        _rmsnorm_kernel,
        grid=(m // tm,),
        in_specs=[pl.BlockSpec((tm, d), lambda i: (i, 0)), _const_spec((1, d))],
        out_specs=pl.BlockSpec((tm, d), lambda i: (i, 0)),
        out_shape=jax.ShapeDtypeStruct((m, d), BF16),
        compiler_params=_params("parallel"),
        name="rmsnorm",
    )(x, g.reshape(1, d))


def _matmul_kernel(a_ref, w_ref, o_ref):
    o_ref[...] = _dot(a_ref[...], w_ref[...])


def matmul(a, w, tm, tn):
    m, k = a.shape
    n = w.shape[1]
    return pl.pallas_call(
        _matmul_kernel,
        grid=(n // tn, m // tm),
        in_specs=[pl.BlockSpec((tm, k), lambda j, i: (i, 0)), pl.BlockSpec((k, tn), lambda j, i: (0, j))],
        out_specs=pl.BlockSpec((tm, tn), lambda j, i: (i, j)),
        out_shape=jax.ShapeDtypeStruct((m, n), F32),
        compiler_params=_params("parallel", "parallel", vmem=VMEM_LIMIT),
        name="matmul",
    )(a, w)


def _proj_kernel(a_ref, wo_ref, x_ref, g_ref, wg_ref, p_ref, wp_ref, *rest):
    m = _dot(a_ref[...].astype(BF16), wo_ref[...])
    ms = jnp.mean(m * m, axis=-1, keepdims=True)
    xn = x_ref[...] + m * lax.rsqrt(ms + EPS) * g_ref[...]
    gate = _sigmoid(_dot(xn.astype(BF16), wg_ref[...]))
    pe = _dot(p_ref[...].astype(BF16), wp_ref[...])
    out = xn + gate * pe
    if len(rest) == 1:
        (o_ref,) = rest
    else:
        gn_ref, o_ref, hn_ref = rest
        ms = jnp.mean(out * out, axis=-1, keepdims=True)
        hn_ref[...] = (out * lax.rsqrt(ms + EPS) * gn_ref[...]).astype(hn_ref.dtype)
    o_ref[...] = out


def proj_residual_ple(a, w_out, x, g_post, w_gate, p, w_proj, tm, g_next=None):
    m, k = a.shape
    d = x.shape[1]
    e = p.shape[1]
    row = lambda i: (i, 0)
    in_specs = [
        pl.BlockSpec((tm, k), row),
        _const_spec((k, d), single_buffer=True),
        pl.BlockSpec((tm, d), row),
        _const_spec((1, d)),
        _const_spec((d, d), single_buffer=True),
        pl.BlockSpec((tm, e), row),
        _const_spec((e, d), single_buffer=True),
    ]
    args = [a, w_out, x, g_post.reshape(1, d), w_gate, p, w_proj]
    out_specs = [pl.BlockSpec((tm, d), row)]
    out_shape = [jax.ShapeDtypeStruct((m, d), F32)]
    if g_next is not None:
        in_specs.append(_const_spec((1, d)))
        args.append(g_next.reshape(1, d))
        out_specs.append(pl.BlockSpec((tm, d), row))
        out_shape.append(jax.ShapeDtypeStruct((m, d), BF16))
    out = pl.pallas_call(
        _proj_kernel,
        grid=(m // tm,),
        in_specs=in_specs,
        out_specs=out_specs,
        out_shape=out_shape,
        compiler_params=_params("parallel", vmem=VMEM_LIMIT),
        name="proj_residual_ple",
    )(*args)
    return out if g_next is not None else (out[0], None)


def _sb_scores(z2, trib, mask):
    tk = z2.shape[1]
    sp = _softplus2(z2)
    drop = sp if mask is None else jnp.where(mask, sp, 0.0)
    cs = _dot(drop.astype(BF16), trib)
    if trib.shape[1] == tk:
        total = jnp.broadcast_to(-jnp.sum(drop, axis=1, keepdims=True), (z2.shape[0], LANES))
        return z2 - sp + cs, total
    return z2 - sp + cs[:, :tk], cs[:, tk:]


def _tri_ones(t, total_width):
    r = lax.broadcasted_iota(jnp.int32, (t, t + total_width), 0)
    c = lax.broadcasted_iota(jnp.int32, (t, t + total_width), 1)
    return jnp.where((r > c) | (c >= t), -1.0, 0.0).astype(BF16)


def _sb_prompt_kernel(bias_ref, q_ref, k_ref, v_ref, g_ref, o_ref, z_ref, w_ref, c_ref, acc_ref,
                      *, scale, tq, tk, dh, hps):
    hb = pl.program_id(1)
    qi = pl.program_id(2)
    nd = tq // tk
    trib = _tri_ones(tk, 0)
    heads = range(hps)
    lanes = [slice(j * dh, (j + 1) * dh) for j in heads]
    bias = [bias_ref[hb * hps + j] * LOG2E for j in heads]
    q = [(q_ref[:, lanes[j]] * (scale * LOG2E)).astype(BF16) for j in heads]
    c_ref[...] = jnp.zeros_like(c_ref)
    acc_ref[...] = jnp.zeros_like(acc_ref)

    n = (qi + 1) * nd

    def kv(ref, i, j):
        kb = jnp.maximum(n - 1 - i, 0)
        return ref[pl.ds(pl.multiple_of(kb * tk, tk), tk), lanes[j]].astype(BF16)

    def logits(i):
        for j in heads:
            z_ref[j] = _dot_nt(q[j], kv(k_ref, i, j)) + bias[j]

    def weights(i, masked):
        mask = None
        if masked:
            row = lax.broadcasted_iota(jnp.int32, (tq, tk), 0)
            col = lax.broadcasted_iota(jnp.int32, (tq, tk), 1)
            mask = col + (nd - 1 - i) * tk < row
        for j in heads:
            logw, total = _sb_scores(z_ref[j], trib, mask)
            c = c_ref[j]
            w = jnp.exp2(logw + jnp.concatenate([c] * (tk // c.shape[1]), axis=1))
            w_ref[j] = (w if mask is None else jnp.where(mask, w, 0.0)).astype(BF16)
            c_ref[j] = c + total

    def accumulate(i):
        for j in heads:
            acc_ref[:, lanes[j]] += _dot(w_ref[j], kv(v_ref, i, j))

    logits(0)
    weights(0, True)
    logits(1)
    for i in range(nd - 1):
        accumulate(i)
        weights(i + 1, True)
        logits(i + 2)

    def body(i, _):
        accumulate(i)
        weights(i + 1, False)
        logits(i + 2)
        return 0

    lax.fori_loop(nd - 1, n, body, 0)
    o_ref[...] = (acc_ref[...] * _silu(g_ref[...])).astype(o_ref.dtype)


def sb_attention_prompt(q, k, v, g, bias, batch, heads, tq, tk, hps):
    n, width = q.shape
    t = n // batch
    dh = width // heads
    nq = t // tq
    assert tq % tk == 0 and tk % LANES == 0 and heads % hps == 0
    qspec = pl.BlockSpec((tq, hps * dh), lambda b, h, i: (b * nq + i, h))
    kvspec = pl.BlockSpec((t, hps * dh), lambda b, h, i: (b, h))
    return pl.pallas_call(
        functools.partial(_sb_prompt_kernel, scale=1.0 / math.sqrt(dh), tq=tq, tk=tk, dh=dh, hps=hps),
        grid=(batch, heads // hps, nq),
        in_specs=[pl.BlockSpec(memory_space=pltpu.SMEM), qspec, kvspec, kvspec, qspec],
        out_specs=qspec,
        out_shape=jax.ShapeDtypeStruct((n, width), BF16),
        scratch_shapes=[pltpu.VMEM((hps, tq, tk), F32), pltpu.VMEM((hps, tq, tk), BF16),
                        pltpu.VMEM((hps, tq, LANES), F32), pltpu.VMEM((tq, hps * dh), F32)],
        compiler_params=_params("parallel", "parallel", "parallel"),
        name="sb_attention_prompt",
    )(bias, q, k, v, g)


def _sb_sample_kernel(pt_ref, qbd_ref, bias_ref, kn_ref, vn_ref, *refs, heads, tq, pages_per_step):
    kc_refs = refs[:pages_per_step]
    vc_refs = refs[pages_per_step:2 * pages_per_step]
    g_ref, o_ref, c_ref, acc_ref = refs[2 * pages_per_step:]
    s = pl.program_id(1)
    page = kn_ref.shape[0] // heads
    dh = kn_ref.shape[1]
    hq = heads * tq
    half = heads // 2
    rows = 2 * page
    ri = lax.broadcasted_iota(jnp.int32, (rows, hq), 0)
    li = lax.broadcasted_iota(jnp.int32, (rows, hq), 1)
    valid = ri % 2 == li // (tq * half)
    r2 = lax.broadcasted_iota(jnp.int32, (rows + SUBLANES, rows), 0)
    c2 = lax.broadcasted_iota(jnp.int32, (rows + SUBLANES, rows), 1)
    trib = jnp.where(((c2 // 2 > r2 // 2) & (c2 % 2 == r2 % 2)) | (r2 >= rows), -1.0, 0.0).astype(BF16)

    def pair_rows(ref, p):
        return ref[pl.ds(p, rows, stride=half), :].astype(BF16)

    def tile(k_ref, v_ref, mask, c, acc):
        k_all = jnp.concatenate([pair_rows(k_ref, p) for p in range(half)], axis=1)
        z2 = bias_ref[...] + _dot(k_all, qbd_ref[...])
        sp = _softplus2(z2)
        keep = valid if mask is None else valid & mask
        cs = _dot(trib, jnp.where(keep, sp, 0.0).astype(BF16))
        w = jnp.where(keep, jnp.exp2(z2 - sp + cs[:rows] + c[0:1, :]), 0.0)
        wt = w.T
        out = [None] * heads
        for p in range(half):
            q = p + half
            lhs = jnp.concatenate([wt[p * tq:(p + 1) * tq], wt[q * tq:(q + 1) * tq]], axis=0).astype(BF16)
            res = _dot(lhs, pair_rows(v_ref, p))
            out[p], out[q] = res[:tq], res[tq:]
        return c + cs[rows:], acc + jnp.concatenate(out, axis=1)

    @pl.when(s == 0)
    def _():
        zero = jnp.zeros(acc_ref.shape, F32)
        c_ref[...], acc_ref[...] = tile(kn_ref, vn_ref, ri // 2 < li % tq, zero[:, :hq], zero)

    @pl.when(s > 0)
    def _():
        c, acc = c_ref[...], acc_ref[...]
        for k_ref, v_ref in zip(kc_refs, vc_refs):
            c, acc = tile(k_ref, v_ref, None, c, acc)
        c_ref[...], acc_ref[...] = c, acc

    @pl.when(s == pl.num_programs(1) - 1)
    def _():
        o_ref[...] = acc_ref[...] * _silu(g_ref[...])


def sb_attention_sample(q, k, v, g, bias, cache_k, cache_v, layer, page_table, pages_per_step):
    seqs, tq, width = q.shape
    _, pool, page, heads, dh = cache_k.shape
    hq = heads * tq
    n_pages = page_table.shape[1]
    assert hq == LANES and tq == SUBLANES and tq <= page and n_pages % pages_per_step == 0
    cache_k = cache_k.reshape(-1, pool, page * heads, dh)
    cache_v = cache_v.reshape(-1, pool, page * heads, dh)
    scale = LOG2E / math.sqrt(dh)
    q4 = (q * scale).reshape(seqs, tq, heads, dh)
    eye = jnp.eye(heads, dtype=F32)
    qbd = jnp.einsum("bqhd,hg->bhdgq", q4, eye).reshape(seqs, 2, width // 2, hq)
    qbd = (qbd[:, 0] + qbd[:, 1]).astype(BF16)
    bias_row = jnp.repeat(bias * LOG2E, tq).reshape(1, hq)
    pad = lambda u: jnp.pad(u, ((0, 0), (0, page - tq), (0, 0))).reshape(seqs, page * heads, dh)
    seq_spec = lambda shape: pl.BlockSpec((None,) + shape, lambda b, s, pt: (b, 0, 0))

    def cache_spec(p):
        def index(b, s, pt):
            return layer, pt[b, n_pages - 1 - ((jnp.maximum(s, 1) - 1) * pages_per_step + p)], 0, 0
        return pl.BlockSpec((None, None, page * heads, dh), index)

    cache_specs = [cache_spec(p) for p in range(pages_per_step)]
    grid_spec = pltpu.PrefetchScalarGridSpec(
        num_scalar_prefetch=1,
        grid=(seqs, n_pages // pages_per_step + 1),
        in_specs=[seq_spec((width // 2, hq)), pl.BlockSpec((1, hq), lambda b, s, pt: (0, 0)),
                  seq_spec((page * heads, dh)), seq_spec((page * heads, dh)), *cache_specs, *cache_specs,
                  seq_spec((tq, width))],
        out_specs=seq_spec((tq, width)),
        scratch_shapes=[pltpu.VMEM((SUBLANES, hq), F32), pltpu.VMEM((tq, width), F32)],
    )
    return pl.pallas_call(
        functools.partial(_sb_sample_kernel, heads=heads, tq=tq, pages_per_step=pages_per_step),
        grid_spec=grid_spec,
        out_shape=jax.ShapeDtypeStruct((seqs, tq, width), F32),
        compiler_params=_params("parallel", "arbitrary", vmem=VMEM_LIMIT),
        name="sb_attention_sample",
    )(page_table, qbd, bias_row, pad(k), pad(v), *([cache_k] * pages_per_step), *([cache_v] * pages_per_step), g)


CV_HIST = 32


def _conv_kernel(val_ref, gate_ref, z_ref, st_ref, w_ref, b_ref, lg_ref, lb_ref, a_ref, ns_ref, e_ref, co_ref,
                 win_ref, *, taps, tt):
    ti = pl.program_id(1)
    off = CV_HIST - (taps - 1)

    @pl.when(ti == 0)
    def _():
        e_ref[0:CV_HIST, :] = st_ref[...]

    e_ref[CV_HIST:CV_HIST + tt, :] = val_ref[...] * _sigmoid(gate_ref[...])
    rb = min(tt, LANES)

    def column_block(ci, _):
        cols = pl.ds(pl.multiple_of(ci * LANES, LANES), LANES)
        for r0 in range(0, tt, rb):
            acc = jnp.broadcast_to(b_ref[:, cols], (rb, LANES))
            for s in range(SUBLANES):
                ws = [w for w in range(taps) if (off + w) % SUBLANES == s]
                if not ws:
                    continue
                span = rb + ws[-1] - ws[0]
                win_ref[0:span, :] = e_ref[pl.ds(off + ws[0] + r0, span), cols]
                for w in ws:
                    acc = acc + win_ref[w - ws[0]:w - ws[0] + rb, :] * w_ref[pl.ds(w, 1), cols]
            co_ref[pl.ds(r0, rb), cols] = acc
        return 0

    lax.fori_loop(0, e_ref.shape[1] // LANES, column_block, 0)
    acc = co_ref[...]
    mu = jnp.mean(acc, axis=-1, keepdims=True)
    cen = acc - mu
    var = jnp.mean(cen * cen, axis=-1, keepdims=True)
    c = _silu(cen * lax.rsqrt(var + EPS) * lg_ref[...] + lb_ref[...])
    a_ref[...] = (c * _silu(z_ref[...])).astype(a_ref.dtype)

    @pl.when(ti == pl.num_programs(1) - 1)
    def _():
        ns_ref[...] = e_ref[tt + off:tt + CV_HIST, :]

    e_ref[0:CV_HIST, :] = e_ref[tt:tt + CV_HIST, :]


def conv_module(val, gate, z, state, w_dw, b_dw, ln_g, ln_b, batch, tt, out_dtype):
    n, c = val.shape
    taps = w_dw.shape[0]
    nt = n // batch // tt
    st = jnp.pad(state, ((0, 0), (CV_HIST - (taps - 1), 0), (0, 0)))
    row = pl.BlockSpec((tt, c), lambda b, i: (b * nt + i, 0))
    vec = _const_spec((1, c))
    return pl.pallas_call(
        functools.partial(_conv_kernel, taps=taps, tt=tt),
        grid=(batch, nt),
        in_specs=[row, row, row, pl.BlockSpec((None, CV_HIST, c), lambda b, i: (b, 0, 0)),
                  _const_spec((taps, c)), vec, vec, vec],
        out_specs=[row, pl.BlockSpec((None, taps - 1, c), lambda b, i: (b, 0, 0))],
        out_shape=[jax.ShapeDtypeStruct((n, c), out_dtype), jax.ShapeDtypeStruct((batch, taps - 1, c), F32)],
        scratch_shapes=[pltpu.VMEM((CV_HIST + tt, c), F32), pltpu.VMEM((tt, c), F32),
                        pltpu.VMEM((min(tt, LANES) + CV_HIST, LANES), F32)],
        compiler_params=_params("parallel", "arbitrary", vmem=VMEM_LIMIT),
        name="conv_module",
    )(val, gate, z, st, w_dw, b_dw.reshape(1, c), ln_g.reshape(1, c), ln_b.reshape(1, c))


SSD_HIST = 8


def _ssd_kernel(z_ref, xbc_ref, dt_ref, cst_ref, sst_ref, wc_ref, bc_ref, dtb_ref, alog_ref, dsk_ref, ng_ref,
                a_ref, ncs_ref, nss_ref, e_ref, act_ref, y_ref, s_ref, acum_t_ref, dt_t_ref,
                *, taps, inner, groups, hpg, hd, t_valid):
    ci = pl.program_id(1)
    L = SSD_CHUNK
    rin = z_ref.shape[0]
    n = SSD_STATE
    gw = hpg * hd
    off = SSD_HIST - (taps - 1)
    cw = 2 * LANES

    @pl.when(ci == 0)
    def _():
        e_ref[0:SSD_HIST, :] = cst_ref[...]
        s_ref[...] = sst_ref[...]

    e_ref[SSD_HIST:SSD_HIST + rin, :] = xbc_ref[...]
    dt_raw = dt_ref[...]
    if rin < L:
        e_ref[SSD_HIST + rin:SSD_HIST + L, :] = jnp.zeros((L - rin, e_ref.shape[1]), F32)
        dt_raw = jnp.concatenate([dt_raw, jnp.zeros((L - rin, LANES), F32)], axis=0)

    def conv_block(bi, _):
        cols = pl.ds(pl.multiple_of(bi * cw, cw), cw)
        conv = jnp.broadcast_to(bc_ref[:, cols], (L, cw))
        for w in range(taps):
            conv = conv + e_ref[pl.ds(off + w, L), cols] * wc_ref[pl.ds(w, 1), cols]
        act_ref[:, cols] = _silu(conv)
        return 0

    lax.fori_loop(0, e_ref.shape[1] // cw, conv_block, 0)

    rows = lax.broadcasted_iota(jnp.int32, (L, LANES), 0) + ci * L
    dt = jnp.where(rows < t_valid, _softplus(dt_raw + dtb_ref[...]), 0.0)
    da = dt * (-jnp.exp(alog_ref[...]))
    r = lax.broadcasted_iota(jnp.int32, (L, L), 0)
    c = lax.broadcasted_iota(jnp.int32, (L, L), 1)
    causal = r >= c
    tril = jnp.where(causal, 1.0, 0.0).astype(BF16)
    acum = sum(_dot(tril, part) for part in _split_bf16(da, 3))
    acum_t_ref[...] = acum.T
    dt_t_ref[...] = dt.T
    lane = lax.broadcasted_iota(jnp.int32, (L, LANES), 1)
    sub = lax.broadcasted_iota(jnp.int32, (LANES, n), 0)
    first = lane < hd
    first_rows = sub < hd

    def group(g, _):
        b_g = act_ref[:, pl.ds(pl.multiple_of(inner + g * n, n), n)].astype(BF16)
        c_g = act_ref[:, pl.ds(pl.multiple_of(inner + groups * n + g * n, n), n)].astype(BF16)
        cb = _dot_nt(c_g, b_g)
        for i in range(hpg // 2):
            h0 = g * hpg + 2 * i
            xcols = pl.ds(pl.multiple_of(g * gw + i * LANES, LANES), LANES)
            xp = act_ref[:, xcols]
            y = dsk_ref[:, xcols] * xp
            cols, rws, lasts = [], [], []
            for j in range(2):
                hh = h0 + j
                a_col = jnp.sum(jnp.where(lane == hh, acum, 0.0), axis=1, keepdims=True)
                a_row = acum_t_ref[pl.ds(hh, 1), :]
                dt_row = dt_t_ref[pl.ds(hh, 1), :]
                dec = jnp.exp(jnp.where(causal, a_col - a_row, -jnp.inf))
                sc = (cb * dec * dt_row).astype(BF16)
                xm = jnp.where(first if j == 0 else ~first, xp, 0.0).astype(BF16)
                y = y + _dot(sc, xm)
                last = a_row[:, L - 1:L]
                cols.append(a_col)
                rws.append(dt_row * jnp.exp(last - a_row))
                lasts.append(jnp.exp(last))
            srows = pl.ds(pl.multiple_of(h0 * hd, LANES), LANES)
            st = s_ref[srows, :]
            y = y + _dot_nt(c_g, st.astype(BF16)) * jnp.exp(jnp.where(first, cols[0], cols[1]))
            y_ref[:, xcols] = y
            xw_t = xp.T * jnp.where(first_rows, rws[0], rws[1])
            s_ref[srows, :] = st * jnp.where(first_rows, lasts[0], lasts[1]) + _dot(xw_t.astype(BF16), b_g)
        return 0

    lax.fori_loop(0, groups, group, 0)

    def gate_block(bi, ss):
        cols = pl.ds(pl.multiple_of(bi * cw, cw), cw)
        yz = y_ref[0:rin, cols] * _silu(z_ref[:, cols])
        y_ref[0:rin, cols] = yz
        return ss + jnp.sum(yz * yz, axis=1, keepdims=True)

    ss = lax.fori_loop(0, inner // cw, gate_block, jnp.zeros((rin, 1), F32))
    a_ref[...] = (y_ref[0:rin, :] * lax.rsqrt(ss / inner + EPS) * ng_ref[...]).astype(a_ref.dtype)

    @pl.when(ci == pl.num_programs(1) - 1)
    def _():
        tv = t_valid - (-(-t_valid // L) - 1) * L
        ncs_ref[...] = e_ref[SSD_HIST + tv - (taps - 1):SSD_HIST + tv, :]
        nss_ref[...] = s_ref[...]

    e_ref[0:SSD_HIST, :] = e_ref[L:L + SSD_HIST, :]


def ssd_module(z, xbc, dt, conv_state, ssm_state, w_conv, b_conv, dt_bias, a_log, d_skip, norm_g, batch):
    nrow, inner = z.shape
    conv_dim = xbc.shape[1]
    taps = w_conv.shape[0]
    heads = dt_bias.shape[0]
    hd = inner // heads
    groups = (conv_dim - inner) // (2 * SSD_STATE)
    hpg = heads // groups
    L = SSD_CHUNK
    t_valid = nrow // batch
    rin = min(L, t_valid)
    nc = t_valid // rin
    assert 2 * hd == LANES and hpg % 2 == 0 and heads <= LANES and L == LANES
    assert t_valid % rin == 0 and rin % SUBLANES == 0 and taps - 1 <= rin
    padh = lambda v: jnp.pad(v, (0, LANES - heads)).reshape(1, LANES)
    cst = jnp.pad(conv_state, ((0, 0), (SSD_HIST - (taps - 1), 0), (0, 0)))
    sst = ssm_state.reshape(batch, heads * hd, SSD_STATE)
    row = lambda w: pl.BlockSpec((rin, w), lambda b, i: (b * nc + i, 0))
    per_b = lambda shape: pl.BlockSpec((None,) + shape, lambda b, i: (b, 0, 0))
    return pl.pallas_call(
        functools.partial(_ssd_kernel, taps=taps, inner=inner, groups=groups, hpg=hpg, hd=hd, t_valid=t_valid),
        grid=(batch, nc),
        in_specs=[row(inner), row(conv_dim), row(LANES), per_b((SSD_HIST, conv_dim)), per_b((heads * hd, SSD_STATE)),
                  _const_spec((taps, conv_dim)), _const_spec((1, conv_dim)), _const_spec((1, LANES)),
                  _const_spec((1, LANES)), _const_spec((1, inner)), _const_spec((1, inner))],
        out_specs=[row(inner), per_b((taps - 1, conv_dim)), per_b((heads * hd, SSD_STATE))],
        out_shape=[jax.ShapeDtypeStruct((nrow, inner), BF16 if rin == L else F32),
                   jax.ShapeDtypeStruct((batch, taps - 1, conv_dim), F32),
                   jax.ShapeDtypeStruct((batch, heads * hd, SSD_STATE), F32)],
        scratch_shapes=[pltpu.VMEM((SSD_HIST + L, conv_dim), F32), pltpu.VMEM((L, conv_dim), F32),
                        pltpu.VMEM((L, inner), F32), pltpu.VMEM((heads * hd, SSD_STATE), F32),
                        pltpu.VMEM((LANES, L), F32), pltpu.VMEM((LANES, L), F32)],
        compiler_params=_params("parallel", "arbitrary", vmem=VMEM_LIMIT),
        name="ssd_module",
    )(z, xbc, dt, cst, sst, w_conv, b_conv.reshape(1, conv_dim), padh(dt_bias), padh(a_log),
      jnp.repeat(d_skip, hd).reshape(1, inner), norm_g.reshape(1, inner))


def _tile(m, pref):
    return pref if m % pref == 0 else m


def kernel(x_prompt, x_sample, cache_k, cache_v, state_conv, state_ssm_conv, state_ssm, page_table, p_prompt, p_sample, norm_pre, norm_post, w_ple_gate, w_ple_proj, w_sb_in, w_sb_out, sb_bias, w_cv_in, w_cv_dw, b_cv_dw, ln_cv_g, ln_cv_b, w_cv_out, w_ssd_in, w_ssd_conv, b_ssd_conv, dt_bias, a_log, d_skip, norm_ssd, w_ssd_out):
    bp, tp, d = x_prompt.shape
    bs, ts, _ = x_sample.shape
    depth = norm_pre.shape[0]
    heads = sb_bias.shape[1]
    sb_width = w_sb_out.shape[1]
    dh = sb_width // heads
    page = cache_k.shape[2]
    cv_width = w_cv_out.shape[1]
    inner = norm_ssd.shape[1]
    conv_dim = w_ssd_conv.shape[2]
    ssd_heads = dt_bias.shape[1]
    L = SSD_CHUNK

    np_, ns_ = bp * tp, bs * ts
    xp = x_prompt.reshape(np_, d)
    xs = x_sample.reshape(ns_, d)
    tmp = _tile(np_, 512)
    tpp = _tile(np_, 256)
    bf = lambda w: w.astype(BF16)

    def project(h, w, tm):
        return matmul(h, w, tm, _tile(w.shape[1], 2048))

    k_pr, v_pr, k_sa, v_sa, cv_pr, cv_sa, sc_pr, sc_sa, ss_pr, ss_sa = ([] for _ in range(10))
    hp = rmsnorm_bf16(xp, norm_pre[0], tmp)
    hs = rmsnorm_bf16(xs, norm_pre[0], ns_)
    for i in range(depth):
        kind, j = i % N_MIXERS, i // N_MIXERS
        if kind == 0:
            w_in = [bf(w_sb_in[j][:, n * sb_width:(n + 1) * sb_width]) for n in range(4)]
            qp, kp, vp, gp = (project(hp, w, tmp) for w in w_in)
            tq = _tile(tp, 512)
            ap = sb_attention_prompt(qp, kp, vp, gp, sb_bias[j], bp, heads, tq, _tile(tq, 256), 1)
            qs, ks, vs, gs = (project(hs, w, ns_).reshape(bs, ts, sb_width) for w in w_in)
            as_ = sb_attention_sample(qs, ks, vs, gs, sb_bias[j], cache_k, cache_v, j, page_table,
                                      math.gcd(page_table.shape[1], 4))
            as_ = as_.reshape(ns_, sb_width)
            w_out = bf(w_sb_out[j])
            k_pr.append(kp.reshape(bp, tp, heads, dh)); v_pr.append(vp.reshape(bp, tp, heads, dh))
            k_sa.append(ks.reshape(bs, ts, heads, dh)); v_sa.append(vs.reshape(bs, ts, heads, dh))
        elif kind == 1:
            w_in = [bf(w_cv_in[j][:, n * cv_width:(n + 1) * cv_width]) for n in range(3)]
            taps = w_cv_dw.shape[1]
            valp, gatep, zp = (project(hp, w, tmp) for w in w_in)
            zero_cv = jnp.zeros((bp, taps - 1, cv_width), F32)
            ap, cp = conv_module(valp, gatep, zp, zero_cv, w_cv_dw[j], b_cv_dw[j], ln_cv_g[j], ln_cv_b[j],
                                 bp, _tile(tp, 256), BF16)
            vals, gates, zs = (project(hs, w, ns_) for w in w_in)
            as_, cs = conv_module(vals, gates, zs, state_conv[j], w_cv_dw[j], b_cv_dw[j], ln_cv_g[j], ln_cv_b[j],
                                  bs, ts, F32)
            w_out = bf(w_cv_out[j])
            cv_pr.append(cp); cv_sa.append(cs)
        else:
            w = w_ssd_in[j]
            w_z = bf(w[:, :inner])
            w_xbc = bf(w[:, inner:inner + conv_dim])
            w_dt = bf(jnp.pad(w[:, inner + conv_dim:], ((0, 0), (0, LANES - ssd_heads))))
            taps = w_ssd_conv.shape[1]
            args = (w_ssd_conv[j], b_ssd_conv[j], dt_bias[j], a_log[j], d_skip[j], norm_ssd[j])
            zp, xbcp, dtp = (project(hp, ww, tmp) for ww in (w_z, w_xbc, w_dt))
            zero_sc = jnp.zeros((bp, taps - 1, conv_dim), F32)
            zero_ss = jnp.zeros((bp, ssd_heads, inner // ssd_heads, SSD_STATE), F32)
            ap, scp, ssp = ssd_module(zp, xbcp, dtp, zero_sc, zero_ss, *args, bp)
            zs, xbcs, dts = (project(hs, ww, ns_) for ww in (w_z, w_xbc, w_dt))
            as_, scs, sss = ssd_module(zs, xbcs, dts, state_ssm_conv[j], state_ssm[j], *args, bs)
            w_out = bf(w_ssd_out[j])
            shape5 = (-1, ssd_heads, inner // ssd_heads, SSD_STATE)
            sc_pr.append(scp); sc_sa.append(scs); ss_pr.append(ssp.reshape(shape5)); ss_sa.append(sss.reshape(shape5))
        w_gate, w_proj = bf(w_ple_gate[i]), bf(w_ple_proj[i])
        g_next = norm_pre[i + 1] if i + 1 < depth else None
        xp, hp = proj_residual_ple(ap, w_out, xp, norm_post[i], w_gate, p_prompt[i].reshape(np_, -1), w_proj, tpp,
                                   g_next)
        xs, hs = proj_residual_ple(as_, w_out, xs, norm_post[i], w_gate, p_sample[i].reshape(ns_, -1), w_proj, ns_,
                                   g_next)
    return (xp.reshape(bp, tp, d), xs.reshape(bs, ts, d), jnp.stack(k_pr), jnp.stack(v_pr), jnp.stack(k_sa),
            jnp.stack(v_sa), jnp.stack(cv_pr), jnp.stack(cv_sa), jnp.stack(sc_pr), jnp.stack(sc_sa),
            jnp.stack(ss_pr), jnp.stack(ss_sa))
```

```python
import functools
import math

import jax
import jax.numpy as jnp
from jax import lax
from jax.experimental import pallas as pl
from jax.experimental.pallas import tpu as pltpu

F32 = jnp.float32
BF16 = jnp.bfloat16
EPS = 1e-6
LOG2E = math.log2(math.e)
N_MIXERS = 3
SSD_STATE = 128
SSD_CHUNK = 128
LANES = 128
SUBLANES = 8
VMEM_LIMIT = 56 * 1024 * 1024


def _params(*sem, vmem=None):
    return pltpu.CompilerParams(dimension_semantics=sem, vmem_limit_bytes=vmem)


def _softplus(z):
    return jnp.maximum(z, 0.0) + jnp.log(1.0 + jnp.exp(-jnp.abs(z)))


def _neg_abs(z):
    sign = jnp.uint32(0x80000000)
    return lax.bitcast_convert_type(lax.bitcast_convert_type(z, jnp.uint32) | sign, F32)


def _softplus2(z2):
    return jnp.maximum(z2, 0.0) + jnp.log(1.0 + jnp.exp2(_neg_abs(z2))) * LOG2E


def _sigmoid(z):
    return 1.0 / (1.0 + jnp.exp(-z))


def _silu(z):
    return z * _sigmoid(z)


def _dot(a, b):
    return jnp.dot(a, b, preferred_element_type=F32)


def _dot_nt(a, b):
    return lax.dot_general(a, b, (((1,), (1,)), ((), ())), preferred_element_type=F32)


def _split_bf16(x, terms):
    parts = []
    r = x
    for t in range(terms):
        p = r.astype(BF16)
        parts.append(p)
        if t + 1 < terms:
            r = r - p.astype(F32)
    return parts


def _const_spec(shape, single_buffer=False):
    zeros = (0,) * len(shape)
    if single_buffer:
        return pl.BlockSpec(shape, lambda *_: zeros, pipeline_mode=pl.Buffered(1))
    return pl.BlockSpec(shape, lambda *_: zeros)


def _rmsnorm_kernel(x_ref, g_ref, o_ref):
    x = x_ref[...]
    ms = jnp.mean(x * x, axis=-1, keepdims=True)
    o_ref[...] = (x * lax.rsqrt(ms + EPS) * g_ref[...]).astype(o_ref.dtype)


def rmsnorm_bf16(x, g, tm):
    m, d = x.shape
    return pl.pallas_call(
        _rmsnorm_kernel,
        grid=(m // tm,),
        in_specs=[pl.BlockSpec((tm, d), lambda i: (i, 0)), _const_spec((1, d))],
        out_specs=pl.BlockSpec((tm, d), lambda i: (i, 0)),
        out_shape=jax.ShapeDtypeStruct((m, d), BF16),
        compiler_params=_params("parallel"),
        name="rmsnorm",
    )(x, g.reshape(1, d))


def _matmul_kernel(a_ref, w_ref, o_ref):
    o_ref[...] = _dot(a_ref[...], w_ref[...])


def matmul(a, w, tm, tn):
    m, k = a.shape
    n = w.shape[1]
    return pl.pallas_call(
        _matmul_kernel,
        grid=(n // tn, m // tm),
        in_specs=[pl.BlockSpec((tm, k), lambda j, i: (i, 0)), pl.BlockSpec((k, tn), lambda j, i: (0, j))],
        out_specs=pl.BlockSpec((tm, tn), lambda j, i: (i, j)),
        out_shape=jax.ShapeDtypeStruct((m, n), F32),
        compiler_params=_params("parallel", "parallel", vmem=VMEM_LIMIT),
        name="matmul",
    )(a, w)


def _proj_kernel(a_ref, wo_ref, x_ref, g_ref, wg_ref, p_ref, wp_ref, *rest):
    m = _dot(a_ref[...].astype(BF16), wo_ref[...])
    ms = jnp.mean(m * m, axis=-1, keepdims=True)
    xn = x_ref[...] + m * lax.rsqrt(ms + EPS) * g_ref[...]
    gate = _sigmoid(_dot(xn.astype(BF16), wg_ref[...]))
    pe = _dot(p_ref[...].astype(BF16), wp_ref[...])
    out = xn + gate * pe
    if len(rest) == 1:
        (o_ref,) = rest
    else:
        gn_ref, o_ref, hn_ref = rest
        ms = jnp.mean(out * out, axis=-1, keepdims=True)
        hn_ref[...] = (out * lax.rsqrt(ms + EPS) * gn_ref[...]).astype(hn_ref.dtype)
    o_ref[...] = out


def proj_residual_ple(a, w_out, x, g_post, w_gate, p, w_proj, tm, g_next=None):
    m, k = a.shape
    d = x.shape[1]
    e = p.shape[1]
    row = lambda i: (i, 0)
    in_specs = [
        pl.BlockSpec((tm, k), row),
        _const_spec((k, d), single_buffer=True),
        pl.BlockSpec((tm, d), row),
        _const_spec((1, d)),
        _const_spec((d, d), single_buffer=True),
        pl.BlockSpec((tm, e), row),
        _const_spec((e, d), single_buffer=True),
    ]
    args = [a, w_out, x, g_post.reshape(1, d), w_gate, p, w_proj]
    out_specs = [pl.BlockSpec((tm, d), row)]
    out_shape = [jax.ShapeDtypeStruct((m, d), F32)]
    if g_next is not None:
        in_specs.append(_const_spec((1, d)))
        args.append(g_next.reshape(1, d))
        out_specs.append(pl.BlockSpec((tm, d), row))
        out_shape.append(jax.ShapeDtypeStruct((m, d), BF16))
    out = pl.pallas_call(
        _proj_kernel,
        grid=(m // tm,),
        in_specs=in_specs,
        out_specs=out_specs,
        out_shape=out_shape,
        compiler_params=_params("parallel", vmem=VMEM_LIMIT),
        name="proj_residual_ple",
    )(*args)
    return out if g_next is not None else (out[0], None)


def _sb_scores(z2, trib, mask):
    tk = z2.shape[1]
    sp = _softplus2(z2)
    drop = sp if mask is None else jnp.where(mask, sp, 0.0)
    cs = _dot(drop.astype(BF16), trib)
    if trib.shape[1] == tk:
        total = jnp.broadcast_to(-jnp.sum(drop, axis=1, keepdims=True), (z2.shape[0], LANES))
        return z2 - sp + cs, total
    return z2 - sp + cs[:, :tk], cs[:, tk:]


def _tri_ones(t, total_width):
    r = lax.broadcasted_iota(jnp.int32, (t, t + total_width), 0)
    c = lax.broadcasted_iota(jnp.int32, (t, t + total_width), 1)
    return jnp.where((r > c) | (c >= t), -1.0, 0.0).astype(BF16)


def _sb_prompt_kernel(bias_ref, q_ref, k_ref, v_ref, g_ref, o_ref, z_ref, w_ref, c_ref, acc_ref,
                      *, scale, tq, tk, dh, hps):
    hb = pl.program_id(1)
    qi = pl.program_id(2)
    nd = tq // tk
    trib = _tri_ones(tk, 0)
    heads = range(hps)
    lanes = [slice(j * dh, (j + 1) * dh) for j in heads]
    bias = [bias_ref[hb * hps + j] * LOG2E for j in heads]
    q = [(q_ref[:, lanes[j]] * (scale * LOG2E)).astype(BF16) for j in heads]
    c_ref[...] = jnp.zeros_like(c_ref)
    acc_ref[...] = jnp.zeros_like(acc_ref)

    n = (qi + 1) * nd

    def kv(ref, i, j):
        kb = jnp.maximum(n - 1 - i, 0)
        return ref[pl.ds(pl.multiple_of(kb * tk, tk), tk), lanes[j]].astype(BF16)

    def logits(i):
        for j in heads:
            z_ref[j] = _dot_nt(q[j], kv(k_ref, i, j)) + bias[j]

    def weights(i, masked):
        mask = None
        if masked:
            row = lax.broadcasted_iota(jnp.int32, (tq, tk), 0)
            col = lax.broadcasted_iota(jnp.int32, (tq, tk), 1)
            mask = col + (nd - 1 - i) * tk < row
        for j in heads:
            logw, total = _sb_scores(z_ref[j], trib, mask)
            c = c_ref[j]
            w = jnp.exp2(logw + jnp.concatenate([c] * (tk // c.shape[1]), axis=1))
            w_ref[j] = (w if mask is None else jnp.where(mask, w, 0.0)).astype(BF16)
            c_ref[j] = c + total

    def accumulate(i):
        for j in heads:
            acc_ref[:, lanes[j]] += _dot(w_ref[j], kv(v_ref, i, j))

    logits(0)
    weights(0, True)
    logits(1)
    for i in range(nd - 1):
        accumulate(i)
        weights(i + 1, True)
        logits(i + 2)

    def body(i, _):
        accumulate(i)
        weights(i + 1, False)
        logits(i + 2)
        return 0

    lax.fori_loop(nd - 1, n, body, 0)
    o_ref[...] = (acc_ref[...] * _silu(g_ref[...])).astype(o_ref.dtype)


def sb_attention_prompt(q, k, v, g, bias, batch, heads, tq, tk, hps):
    n, width = q.shape
    t = n // batch
    dh = width // heads
    nq = t // tq
    assert tq % tk == 0 and tk % LANES == 0 and heads % hps == 0
    qspec = pl.BlockSpec((tq, hps * dh), lambda b, h, i: (b * nq + i, h))
    kvspec = pl.BlockSpec((t, hps * dh), lambda b, h, i: (b, h))
    return pl.pallas_call(
        functools.partial(_sb_prompt_kernel, scale=1.0 / math.sqrt(dh), tq=tq, tk=tk, dh=dh, hps=hps),
        grid=(batch, heads // hps, nq),
        in_specs=[pl.BlockSpec(memory_space=pltpu.SMEM), qspec, kvspec, kvspec, qspec],
        out_specs=qspec,
        out_shape=jax.ShapeDtypeStruct((n, width), BF16),
        scratch_shapes=[pltpu.VMEM((hps, tq, tk), F32), pltpu.VMEM((hps, tq, tk), BF16),
                        pltpu.VMEM((hps, tq, LANES), F32), pltpu.VMEM((tq, hps * dh), F32)],
        compiler_params=_params("parallel", "parallel", "parallel"),
        name="sb_attention_prompt",
    )(bias, q, k, v, g)


def _sb_sample_kernel(pt_ref, qbd_ref, bias_ref, kn_ref, vn_ref, *refs, heads, tq, pages_per_step):
    kc_refs = refs[:pages_per_step]
    vc_refs = refs[pages_per_step:2 * pages_per_step]
    g_ref, o_ref, c_ref, acc_ref = refs[2 * pages_per_step:]
    s = pl.program_id(1)
    page = kn_ref.shape[0] // heads
    dh = kn_ref.shape[1]
    hq = heads * tq
    half = heads // 2
    rows = 2 * page
    ri = lax.broadcasted_iota(jnp.int32, (rows, hq), 0)
    li = lax.broadcasted_iota(jnp.int32, (rows, hq), 1)
    valid = ri % 2 == li // (tq * half)
    r2 = lax.broadcasted_iota(jnp.int32, (rows + SUBLANES, rows), 0)
    c2 = lax.broadcasted_iota(jnp.int32, (rows + SUBLANES, rows), 1)
    trib = jnp.where(((c2 // 2 > r2 // 2) & (c2 % 2 == r2 % 2)) | (r2 >= rows), -1.0, 0.0).astype(BF16)

    def pair_rows(ref, p):
        return ref[pl.ds(p, rows, stride=half), :].astype(BF16)

    def tile(k_ref, v_ref, mask, c, acc):
        k_all = jnp.concatenate([pair_rows(k_ref, p) for p in range(half)], axis=1)
        z2 = bias_ref[...] + _dot(k_all, qbd_ref[...])
        sp = _softplus2(z2)
        keep = valid if mask is None else valid & mask
        cs = _dot(trib, jnp.where(keep, sp, 0.0).astype(BF16))
        w = jnp.where(keep, jnp.exp2(z2 - sp + cs[:rows] + c[0:1, :]), 0.0)
        wt = w.T
        out = [None] * heads
        for p in range(half):
            q = p + half
            lhs = jnp.concatenate([wt[p * tq:(p + 1) * tq], wt[q * tq:(q + 1) * tq]], axis=0).astype(BF16)
            res = _dot(lhs, pair_rows(v_ref, p))
            out[p], out[q] = res[:tq], res[tq:]
        return c + cs[rows:], acc + jnp.concatenate(out, axis=1)

    @pl.when(s == 0)
    def _():
        zero = jnp.zeros(acc_ref.shape, F32)
        c_ref[...], acc_ref[...] = tile(kn_ref, vn_ref, ri // 2 < li % tq, zero[:, :hq], zero)

    @pl.when(s > 0)
    def _():
        c, acc = c_ref[...], acc_ref[...]
        for k_ref, v_ref in zip(kc_refs, vc_refs):
            c, acc = tile(k_ref, v_ref, None, c, acc)
        c_ref[...], acc_ref[...] = c, acc

    @pl.when(s == pl.num_programs(1) - 1)
    def _():
        o_ref[...] = acc_ref[...] * _silu(g_ref[...])


def sb_attention_sample(q, k, v, g, bias, cache_k, cache_v, layer, page_table, pages_per_step):
    seqs, tq, width = q.shape
    _, pool, page, heads, dh = cache_k.shape
    hq = heads * tq
    n_pages = page_table.shape[1]
    assert hq == LANES and tq == SUBLANES and tq <= page and n_pages % pages_per_step == 0
    cache_k = cache_k.reshape(-1, pool, page * heads, dh)
    cache_v = cache_v.reshape(-1, pool, page * heads, dh)
    scale = LOG2E / math.sqrt(dh)
    q4 = (q * scale).reshape(seqs, tq, heads, dh)
    eye = jnp.eye(heads, dtype=F32)
    qbd = jnp.einsum("bqhd,hg->bhdgq", q4, eye).reshape(seqs, 2, width // 2, hq)
    qbd = (qbd[:, 0] + qbd[:, 1]).astype(BF16)
    bias_row = jnp.repeat(bias * LOG2E, tq).reshape(1, hq)
    pad = lambda u: jnp.pad(u, ((0, 0), (0, page - tq), (0, 0))).reshape(seqs, page * heads, dh)
    seq_spec = lambda shape: pl.BlockSpec((None,) + shape, lambda b, s, pt: (b, 0, 0))

    def cache_spec(p):
        def index(b, s, pt):
            return layer, pt[b, n_pages - 1 - ((jnp.maximum(s, 1) - 1) * pages_per_step + p)], 0, 0
        return pl.BlockSpec((None, None, page * heads, dh), index)

    cache_specs = [cache_spec(p) for p in range(pages_per_step)]
    grid_spec = pltpu.PrefetchScalarGridSpec(
        num_scalar_prefetch=1,
        grid=(seqs, n_pages // pages_per_step + 1),
        in_specs=[seq_spec((width // 2, hq)), pl.BlockSpec((1, hq), lambda b, s, pt: (0, 0)),
                  seq_spec((page * heads, dh)), seq_spec((page * heads, dh)), *cache_specs, *cache_specs,
                  seq_spec((tq, width))],
        out_specs=seq_spec((tq, width)),
        scratch_shapes=[pltpu.VMEM((SUBLANES, hq), F32), pltpu.VMEM((tq, width), F32)],
    )
    return pl.pallas_call(
        functools.partial(_sb_sample_kernel, heads=heads, tq=tq, pages_per_step=pages_per_step),
        grid_spec=grid_spec,
        out_shape=jax.ShapeDtypeStruct((seqs, tq, width), F32),
        compiler_params=_params("parallel", "arbitrary", vmem=VMEM_LIMIT),
        name="sb_attention_sample",
    )(page_table, qbd, bias_row, pad(k), pad(v), *([cache_k] * pages_per_step), *([cache_v] * pages_per_step), g)


CV_HIST = 32


def _conv_kernel(val_ref, gate_ref, z_ref, st_ref, w_ref, b_ref, lg_ref, lb_ref, a_ref, ns_ref, e_ref, co_ref,
                 win_ref, *, taps, tt):
    ti = pl.program_id(1)
    off = CV_HIST - (taps - 1)

    @pl.when(ti == 0)
    def _():
        e_ref[0:CV_HIST, :] = st_ref[...]

    e_ref[CV_HIST:CV_HIST + tt, :] = val_ref[...] * _sigmoid(gate_ref[...])
    rb = min(tt, LANES)

    def column_block(ci, _):
        cols = pl.ds(pl.multiple_of(ci * LANES, LANES), LANES)
        for r0 in range(0, tt, rb):
            acc = jnp.broadcast_to(b_ref[:, cols], (rb, LANES))
            for s in range(SUBLANES):
                ws = [w for w in range(taps) if (off + w) % SUBLANES == s]
                if not ws:
                    continue
                span = rb + ws[-1] - ws[0]
                win_ref[0:span, :] = e_ref[pl.ds(off + ws[0] + r0, span), cols]
                for w in ws:
                    acc = acc + win_ref[w - ws[0]:w - ws[0] + rb, :] * w_ref[pl.ds(w, 1), cols]
            co_ref[pl.ds(r0, rb), cols] = acc
        return 0

    lax.fori_loop(0, e_ref.shape[1] // LANES, column_block, 0)
    acc = co_ref[...]
    mu = jnp.mean(acc, axis=-1, keepdims=True)
    cen = acc - mu
    var = jnp.mean(cen * cen, axis=-1, keepdims=True)
    c = _silu(cen * lax.rsqrt(var + EPS) * lg_ref[...] + lb_ref[...])
    a_ref[...] = (c * _silu(z_ref[...])).astype(a_ref.dtype)

    @pl.when(ti == pl.num_programs(1) - 1)
    def _():
        ns_ref[...] = e_ref[tt + off:tt + CV_HIST, :]

    e_ref[0:CV_HIST, :] = e_ref[tt:tt + CV_HIST, :]


def conv_module(val, gate, z, state, w_dw, b_dw, ln_g, ln_b, batch, tt, out_dtype):
    n, c = val.shape
    taps = w_dw.shape[0]
    nt = n // batch // tt
    st = jnp.pad(state, ((0, 0), (CV_HIST - (taps - 1), 0), (0, 0)))
    row = pl.BlockSpec((tt, c), lambda b, i: (b * nt + i, 0))
    vec = _const_spec((1, c))
    return pl.pallas_call(
        functools.partial(_conv_kernel, taps=taps, tt=tt),
        grid=(batch, nt),
        in_specs=[row, row, row, pl.BlockSpec((None, CV_HIST, c), lambda b, i: (b, 0, 0)),
                  _const_spec((taps, c)), vec, vec, vec],
        out_specs=[row, pl.BlockSpec((None, taps - 1, c), lambda b, i: (b, 0, 0))],
        out_shape=[jax.ShapeDtypeStruct((n, c), out_dtype), jax.ShapeDtypeStruct((batch, taps - 1, c), F32)],
        scratch_shapes=[pltpu.VMEM((CV_HIST + tt, c), F32), pltpu.VMEM((tt, c), F32),
                        pltpu.VMEM((min(tt, LANES) + CV_HIST, LANES), F32)],
        compiler_params=_params("parallel", "arbitrary", vmem=VMEM_LIMIT),
        name="conv_module",
    )(val, gate, z, st, w_dw, b_dw.reshape(1, c), ln_g.reshape(1, c), ln_b.reshape(1, c))


SSD_HIST = 8


def _ssd_kernel(z_ref, xbc_ref, dt_ref, cst_ref, sst_ref, wc_ref, bc_ref, dtb_ref, alog_ref, dsk_ref, ng_ref,
                a_ref, ncs_ref, nss_ref, e_ref, act_ref, y_ref, s_ref, acum_t_ref, dt_t_ref,
                *, taps, inner, groups, hpg, hd, t_valid):
    ci = pl.program_id(1)
    L = SSD_CHUNK
    rin = z_ref.shape[0]
    n = SSD_STATE
    gw = hpg * hd
    off = SSD_HIST - (taps - 1)
    cw = 2 * LANES

    @pl.when(ci == 0)
    def _():
        e_ref[0:SSD_HIST, :] = cst_ref[...]
        s_ref[...] = sst_ref[...]

    e_ref[SSD_HIST:SSD_HIST + rin, :] = xbc_ref[...]
    dt_raw = dt_ref[...]
    if rin < L:
        e_ref[SSD_HIST + rin:SSD_HIST + L, :] = jnp.zeros((L - rin, e_ref.shape[1]), F32)
        dt_raw = jnp.concatenate([dt_raw, jnp.zeros((L - rin, LANES), F32)], axis=0)

    def conv_block(bi, _):
        cols = pl.ds(pl.multiple_of(bi * cw, cw), cw)
        conv = jnp.broadcast_to(bc_ref[:, cols], (L, cw))
        for w in range(taps):
            conv = conv + e_ref[pl.ds(off + w, L), cols] * wc_ref[pl.ds(w, 1), cols]
        act_ref[:, cols] = _silu(conv)
        return 0

    lax.fori_loop(0, e_ref.shape[1] // cw, conv_block, 0)

    rows = lax.broadcasted_iota(jnp.int32, (L, LANES), 0) + ci * L
    dt = jnp.where(rows < t_valid, _softplus(dt_raw + dtb_ref[...]), 0.0)
    da = dt * (-jnp.exp(alog_ref[...]))
    r = lax.broadcasted_iota(jnp.int32, (L, L), 0)
    c = lax.broadcasted_iota(jnp.int32, (L, L), 1)
    causal = r >= c
    tril = jnp.where(causal, 1.0, 0.0).astype(BF16)
    acum = sum(_dot(tril, part) for part in _split_bf16(da, 3))
    acum_t_ref[...] = acum.T
    dt_t_ref[...] = dt.T
    lane = lax.broadcasted_iota(jnp.int32, (L, LANES), 1)
    sub = lax.broadcasted_iota(jnp.int32, (LANES, n), 0)
    first = lane < hd
    first_rows = sub < hd

    def group(g, _):
        b_g = act_ref[:, pl.ds(pl.multiple_of(inner + g * n, n), n)].astype(BF16)
        c_g = act_ref[:, pl.ds(pl.multiple_of(inner + groups * n + g * n, n), n)].astype(BF16)
        cb = _dot_nt(c_g, b_g)
        for i in range(hpg // 2):
            h0 = g * hpg + 2 * i
            xcols = pl.ds(pl.multiple_of(g * gw + i * LANES, LANES), LANES)
            xp = act_ref[:, xcols]
            y = dsk_ref[:, xcols] * xp
            cols, rws, lasts = [], [], []
            for j in range(2):
                hh = h0 + j
                a_col = jnp.sum(jnp.where(lane == hh, acum, 0.0), axis=1, keepdims=True)
                a_row = acum_t_ref[pl.ds(hh, 1), :]
                dt_row = dt_t_ref[pl.ds(hh, 1), :]
                dec = jnp.exp(jnp.where(causal, a_col - a_row, -jnp.inf))
                sc = (cb * dec * dt_row).astype(BF16)
                xm = jnp.where(first if j == 0 else ~first, xp, 0.0).astype(BF16)
                y = y + _dot(sc, xm)
                last = a_row[:, L - 1:L]
                cols.append(a_col)
                rws.append(dt_row * jnp.exp(last - a_row))
                lasts.append(jnp.exp(last))
            srows = pl.ds(pl.multiple_of(h0 * hd, LANES), LANES)
            st = s_ref[srows, :]
            y = y + _dot_nt(c_g, st.astype(BF16)) * jnp.exp(jnp.where(first, cols[0], cols[1]))
            y_ref[:, xcols] = y
            xw_t = xp.T * jnp.where(first_rows, rws[0], rws[1])
            s_ref[srows, :] = st * jnp.where(first_rows, lasts[0], lasts[1]) + _dot(xw_t.astype(BF16), b_g)
        return 0

    lax.fori_loop(0, groups, group, 0)

    def gate_block(bi, ss):
        cols = pl.ds(pl.multiple_of(bi * cw, cw), cw)
        yz = y_ref[0:rin, cols] * _silu(z_ref[:, cols])
        y_ref[0:rin, cols] = yz
        return ss + jnp.sum(yz * yz, axis=1, keepdims=True)

    ss = lax.fori_loop(0, inner // cw, gate_block, jnp.zeros((rin, 1), F32))
    a_ref[...] = (y_ref[0:rin, :] * lax.rsqrt(ss / inner + EPS) * ng_ref[...]).astype(a_ref.dtype)

    @pl.when(ci == pl.num_programs(1) - 1)
    def _():
        tv = t_valid - (-(-t_valid // L) - 1) * L
        ncs_ref[...] = e_ref[SSD_HIST + tv - (taps - 1):SSD_HIST + tv, :]
        nss_ref[...] = s_ref[...]

    e_ref[0:SSD_HIST, :] = e_ref[L:L + SSD_HIST, :]


def ssd_module(z, xbc, dt, conv_state, ssm_state, w_conv, b_conv, dt_bias, a_log, d_skip, norm_g, batch):
    nrow, inner = z.shape
    conv_dim = xbc.shape[1]
    taps = w_conv.shape[0]
    heads = dt_bias.shape[0]
    hd = inner // heads
    groups = (conv_dim - inner) // (2 * SSD_STATE)
    hpg = heads // groups
    L = SSD_CHUNK
    t_valid = nrow // batch
    rin = min(L, t_valid)
    nc = t_valid // rin
    assert 2 * hd == LANES and hpg % 2 == 0 and heads <= LANES and L == LANES
    assert t_valid % rin == 0 and rin % SUBLANES == 0 and taps - 1 <= rin
    padh = lambda v: jnp.pad(v, (0, LANES - heads)).reshape(1, LANES)
    cst = jnp.pad(conv_state, ((0, 0), (SSD_HIST - (taps - 1), 0), (0, 0)))
    sst = ssm_state.reshape(batch, heads * hd, SSD_STATE)
    row = lambda w: pl.BlockSpec((rin, w), lambda b, i: (b * nc + i, 0))
    per_b = lambda shape: pl.BlockSpec((None,) + shape, lambda b, i: (b, 0, 0))
    return pl.pallas_call(
        functools.partial(_ssd_kernel, taps=taps, inner=inner, groups=groups, hpg=hpg, hd=hd, t_valid=t_valid),
        grid=(batch, nc),
        in_specs=[row(inner), row(conv_dim), row(LANES), per_b((SSD_HIST, conv_dim)), per_b((heads * hd, SSD_STATE)),
                  _const_spec((taps, conv_dim)), _const_spec((1, conv_dim)), _const_spec((1, LANES)),
                  _const_spec((1, LANES)), _const_spec((1, inner)), _const_spec((1, inner))],
        out_specs=[row(inner), per_b((taps - 1, conv_dim)), per_b((heads * hd, SSD_STATE))],
        out_shape=[jax.ShapeDtypeStruct((nrow, inner), BF16 if rin == L else F32),
                   jax.ShapeDtypeStruct((batch, taps - 1, conv_dim), F32),
                   jax.ShapeDtypeStruct((batch, heads * hd, SSD_STATE), F32)],
        scratch_shapes=[pltpu.VMEM((SSD_HIST + L, conv_dim), F32), pltpu.VMEM((L, conv_dim), F32),
                        pltpu.VMEM((L, inner), F32), pltpu.VMEM((heads * hd, SSD_STATE), F32),
                        pltpu.VMEM((LANES, L), F32), pltpu.VMEM((LANES, L), F32)],
        compiler_params=_params("parallel", "arbitrary", vmem=VMEM_LIMIT),
        name="ssd_module",
    )(z, xbc, dt, cst, sst, w_conv, b_conv.reshape(1, conv_dim), padh(dt_bias), padh(a_log),
      jnp.repeat(d_skip, hd).reshape(1, inner), norm_g.reshape(1, inner))


def _tile(m, pref):
    return pref if m % pref == 0 else m


def kernel(x_prompt, x_sample, cache_k, cache_v, state_conv, state_ssm_conv, state_ssm, page_table, p_prompt, p_sample, norm_pre, norm_post, w_ple_gate, w_ple_proj, w_sb_in, w_sb_out, sb_bias, w_cv_in, w_cv_dw, b_cv_dw, ln_cv_g, ln_cv_b, w_cv_out, w_ssd_in, w_ssd_conv, b_ssd_conv, dt_bias, a_log, d_skip, norm_ssd, w_ssd_out):
    bp, tp, d = x_prompt.shape
    bs, ts, _ = x_sample.shape
    depth = norm_pre.shape[0]
    heads = sb_bias.shape[1]
    sb_width = w_sb_out.shape[1]
    dh = sb_width // heads
    page = cache_k.shape[2]
    cv_width = w_cv_out.shape[1]
    inner = norm_ssd.shape[1]
    conv_dim = w_ssd_conv.shape[2]
    ssd_heads = dt_bias.shape[1]
    L = SSD_CHUNK

    np_, ns_ = bp * tp, bs * ts
    xp = x_prompt.reshape(np_, d)
    xs = x_sample.reshape(ns_, d)
    tmp = _tile(np_, 512)
    tpp = _tile(np_, 256)
    bf = lambda w: w.astype(BF16)

    def project(h, w, tm):
        return matmul(h, w, tm, _tile(w.shape[1], 2048))

    k_pr, v_pr, k_sa, v_sa, cv_pr, cv_sa, sc_pr, sc_sa, ss_pr, ss_sa = ([] for _ in range(10))
    hp = rmsnorm_bf16(xp, norm_pre[0], tmp)
    hs = rmsnorm_bf16(xs, norm_pre[0], ns_)
    for i in range(depth):
        kind, j = i % N_MIXERS, i // N_MIXERS
        if kind == 0:
            w_in = [bf(w_sb_in[j][:, n * sb_width:(n + 1) * sb_width]) for n in range(4)]
            qp, kp, vp, gp = (project(hp, w, tmp) for w in w_in)
            tq = _tile(tp, 512)
            ap = sb_attention_prompt(qp, kp, vp, gp, sb_bias[j], bp, heads, tq, _tile(tq, 256), 1)
            qs, ks, vs, gs = (project(hs, w, ns_).reshape(bs, ts, sb_width) for w in w_in)
            as_ = sb_attention_sample(qs, ks, vs, gs, sb_bias[j], cache_k, cache_v, j, page_table,
                                      math.gcd(page_table.shape[1], 8))
            as_ = as_.reshape(ns_, sb_width)
            w_out = bf(w_sb_out[j])
            k_pr.append(kp.reshape(bp, tp, heads, dh)); v_pr.append(vp.reshape(bp, tp, heads, dh))
            k_sa.append(ks.reshape(bs, ts, heads, dh)); v_sa.append(vs.reshape(bs, ts, heads, dh))
        elif kind == 1:
            w_in = [bf(w_cv_in[j][:, n * cv_width:(n + 1) * cv_width]) for n in range(3)]
            taps = w_cv_dw.shape[1]
            valp, gatep, zp = (project(hp, w, tmp) for w in w_in)
            zero_cv = jnp.zeros((bp, taps - 1, cv_width), F32)
            ap, cp = conv_module(valp, gatep, zp, zero_cv, w_cv_dw[j], b_cv_dw[j], ln_cv_g[j], ln_cv_b[j],
                                 bp, _tile(tp, 256), BF16)
            vals, gates, zs = (project(hs, w, ns_) for w in w_in)
            as_, cs = conv_module(vals, gates, zs, state_conv[j], w_cv_dw[j], b_cv_dw[j], ln_cv_g[j], ln_cv_b[j],
                                  bs, ts, F32)
            w_out = bf(w_cv_out[j])
            cv_pr.append(cp); cv_sa.append(cs)
        else:
            w = w_ssd_in[j]
            w_z = bf(w[:, :inner])
            w_xbc = bf(w[:, inner:inner + conv_dim])
            w_dt = bf(jnp.pad(w[:, inner + conv_dim:], ((0, 0), (0, LANES - ssd_heads))))
            taps = w_ssd_conv.shape[1]
            args = (w_ssd_conv[j], b_ssd_conv[j], dt_bias[j], a_log[j], d_skip[j], norm_ssd[j])
            zp, xbcp, dtp = (project(hp, ww, tmp) for ww in (w_z, w_xbc, w_dt))
            zero_sc = jnp.zeros((bp, taps - 1, conv_dim), F32)
            zero_ss = jnp.zeros((bp, ssd_heads, inner // ssd_heads, SSD_STATE), F32)
            ap, scp, ssp = ssd_module(zp, xbcp, dtp, zero_sc, zero_ss, *args, bp)
            zs, xbcs, dts = (project(hs, ww, ns_) for ww in (w_z, w_xbc, w_dt))
            as_, scs, sss = ssd_module(zs, xbcs, dts, state_ssm_conv[j], state_ssm[j], *args, bs)
            w_out = bf(w_ssd_out[j])
            shape5 = (-1, ssd_heads, inner // ssd_heads, SSD_STATE)
            sc_pr.append(scp); sc_sa.append(scs); ss_pr.append(ssp.reshape(shape5)); ss_sa.append(sss.reshape(shape5))
        w_gate, w_proj = bf(w_ple_gate[i]), bf(w_ple_proj[i])
        g_next = norm_pre[i + 1] if i + 1 < depth else None
        xp, hp = proj_residual_ple(ap, w_out, xp, norm_post[i], w_gate, p_prompt[i].reshape(np_, -1), w_proj, tpp,
                                   g_next)
        xs, hs = proj_residual_ple(as_, w_out, xs, norm_post[i], w_gate, p_sample[i].reshape(ns_, -1), w_proj, ns_,
                                   g_next)
    return (xp.reshape(bp, tp, d), xs.reshape(bs, ts, d), jnp.stack(k_pr), jnp.stack(v_pr), jnp.stack(k_sa),
            jnp.stack(v_sa), jnp.stack(cv_pr), jnp.stack(cv_sa), jnp.stack(sc_pr), jnp.stack(sc_sa),
            jnp.stack(ss_pr), jnp.stack(ss_sa))
```
